```python
import math
import jax, jax.numpy as jnp
from jax import lax
import numpy as np

D_MODEL = 1024
BATCH = 32
SEQ = 2048
DEPTH = 1
DEC_BATCH = 16
DEC_SEQ = 2048
PAST_LEN = 128

H_GDN = 8
DK_GDN = 128
DV_GDN = 128
W_GDN = H_GDN * DV_GDN
CONV_K = 5
CHUNK = 64
H_MLA = 8
Q_LORA = 384
KV_LORA = 256
D_NOPE = 128
D_ROPE = 64
D_V_MLA = 128
W_MLA = H_MLA * D_V_MLA
ROPE_THETA = 10000.0
Q_BLOCK = 128
H_MEM = 4
D_MEM = 128
W_MEM = H_MEM * D_MEM
N_MEM = 256
N_BRANCH = 3
EPS = 1e-6
W_QKV = H_GDN * (2 * DK_GDN + DV_GDN)
SPLIT_SIZES = (W_QKV, 4 * H_GDN, W_GDN, Q_LORA, KV_LORA + D_ROPE, W_MLA, W_MEM, W_MEM, N_BRANCH * D_MODEL)
D_IN = sum(SPLIT_SIZES)

kernel_name = "hybrid_gdn_mla_memory_encoder"


def _rmsnorm(x, gain):
    xf = x.astype(jnp.float32)
    y = xf * lax.rsqrt(jnp.mean(xf * xf, axis=-1, keepdims=True) + EPS)
    return (y * gain.astype(jnp.float32)).astype(x.dtype)


def _l2norm(x):
    xf = x.astype(jnp.float32)
    return xf * lax.rsqrt(jnp.sum(xf * xf, axis=-1, keepdims=True) + EPS)


def _centred_conv_silu(x, w):
    c = x.shape[-1]
    y = lax.conv_general_dilated(
        x, w[:, None, :].astype(x.dtype), window_strides=(1,),
        padding=[((CONV_K - 1) // 2, CONV_K // 2)],
        dimension_numbers=("NWC", "WIO", "NWC"), feature_group_count=c)
    return jax.nn.silu(y)


def _rope(x, pos):
    half = x.shape[-1] // 2
    inv_freq = ROPE_THETA ** (-jnp.arange(half, dtype=jnp.float32) / half)
    ang = pos.astype(jnp.float32)[:, None] * inv_freq[None, :]
    cos = jnp.cos(ang)[None, :, None, :]
    sin = jnp.sin(ang)[None, :, None, :]
    xf = x.astype(jnp.float32)
    x1, x2 = xf[..., :half], xf[..., half:]
    return jnp.concatenate([x1 * cos - x2 * sin, x2 * cos + x1 * sin], axis=-1).astype(x.dtype)


def _gdn_chunked(q, k, v, g, beta):
    b, s, h, dk = q.shape
    dv = v.shape[-1]
    n = s // CHUNK

    def chunks(t):
        return t.reshape(b, n, CHUNK, h, -1).transpose(1, 0, 3, 2, 4)

    qc = chunks(q * dk ** -0.5)
    kc = chunks(k)
    vc = chunks(v)
    gc = jnp.cumsum(g.reshape(b, n, CHUNK, h).transpose(1, 0, 3, 2), axis=-1)
    bc = beta.reshape(b, n, CHUNK, h).transpose(1, 0, 3, 2)[..., None]
    lower = jnp.tril(jnp.ones((CHUNK, CHUNK), dtype=bool))
    decay = jnp.exp(jnp.where(lower, gc[..., :, None] - gc[..., None, :], -jnp.inf))
    kb = kc * bc
    eye = jnp.eye(CHUNK, dtype=jnp.float32)
    lmat = jnp.einsum("nbhck,nbhek->nbhce", kb, kc) * decay * (1.0 - eye)
    t_inv = lax.linalg.triangular_solve(lmat + eye, jnp.broadcast_to(eye, lmat.shape),
                                        left_side=True, lower=True, unit_diagonal=True)
    u = jnp.einsum("nbhce,nbhev->nbhcv", t_inv, vc * bc)
    w = jnp.einsum("nbhce,nbhek->nbhck", t_inv, kb * jnp.exp(gc)[..., None])

    def step(state, xs):
        qi, ki, ui, wi, gi, di = xs
        v_new = ui - jnp.einsum("bhck,bhkv->bhcv", wi, state)
        intra = jnp.einsum("bhck,bhek->bhce", qi, ki) * di
        o = (jnp.einsum("bhck,bhkv->bhcv", qi * jnp.exp(gi)[..., None], state)
             + jnp.einsum("bhce,bhev->bhcv", intra, v_new))
        g_last = gi[..., -1]
        state = (state * jnp.exp(g_last)[..., None, None]
                 + jnp.einsum("bhck,bhcv->bhkv", ki * jnp.exp(g_last[..., None] - gi)[..., None], v_new))
        return state, o

    state0 = jnp.zeros((b, h, dk, dv), jnp.float32)
    _, o = lax.scan(step, state0, (qc, kc, u, w, gc, decay))
    return o.transpose(1, 0, 3, 2, 4).reshape(b, s, h, dv)


def _blocked_attention(q, k, v, scale):
    b, s, h, dq = q.shape
    nb = s // Q_BLOCK
    qb = q.reshape(b, nb, Q_BLOCK, h, dq).transpose(1, 0, 2, 3, 4)

    def one_block(qi):
        sc = jnp.einsum("bqhd,bkhd->bhqk", qi, k).astype(jnp.float32) * scale
        p = jax.nn.softmax(sc, axis=-1)
        return jnp.einsum("bhqk,bkhd->bqhd", p.astype(v.dtype), v)

    o = lax.map(one_block, qb)
    return o.transpose(1, 0, 2, 3, 4).reshape(b, s, h, v.shape[-1])


def _layer(x, mem, pos, attn_norm_gain, w_in, conv_w, a_log, dt_bias, gdn_norm_gain,
           q_norm_gain, w_q_up, kv_norm_gain, w_kv_up, mem_norm_gain, w_mem_kv,
           w_br_gdn, w_br_mla, w_br_mem, w_out):
    b, s, _ = x.shape
    h = _rmsnorm(x, attn_norm_gain)
    proj = h @ w_in
    idx = [int(i) for i in np.cumsum(SPLIT_SIZES)[:-1]]
    qkv, ab, z_gdn, cq, ckv, z_mla, q_mem, z_mem, gate_logits = jnp.split(proj, idx, axis=-1)

    qkv = _centred_conv_silu(qkv, conv_w)
    q, k, v = jnp.split(qkv, [H_GDN * DK_GDN, 2 * H_GDN * DK_GDN], axis=-1)
    q = _l2norm(q.reshape(b, s, H_GDN, DK_GDN))
    k = _l2norm(k.reshape(b, s, H_GDN, DK_GDN))
    v = v.reshape(b, s, H_GDN, DV_GDN).astype(jnp.float32)
    ab = ab.astype(jnp.float32).reshape(b, s, 4, H_GDN)
    g = -jnp.exp(a_log.astype(jnp.float32)) * jax.nn.softplus(ab[:, :, 0:2] + dt_bias.astype(jnp.float32))
    beta = jax.nn.sigmoid(ab[:, :, 2:4])
    o_fwd = _gdn_chunked(q, k, v, g[:, :, 0], beta[:, :, 0])
    o_bwd = _gdn_chunked(q[:, ::-1], k[:, ::-1], v[:, ::-1], g[:, ::-1, 1], beta[:, ::-1, 1])[:, ::-1]
    o_gdn = _rmsnorm(o_fwd + o_bwd, gdn_norm_gain).reshape(b, s, W_GDN).astype(x.dtype) * jax.nn.silu(z_gdn)

    q_lat = _rmsnorm(cq, q_norm_gain)
    qm = (q_lat @ w_q_up).reshape(b, s, H_MLA, D_NOPE + D_ROPE)
    q_nope, q_pe = jnp.split(qm, [D_NOPE], axis=-1)
    q_pe = _rope(q_pe, pos)
    kv_lat, k_pe = jnp.split(ckv, [KV_LORA], axis=-1)
    kv_lat = _rmsnorm(kv_lat, kv_norm_gain)
    k_pe = _rope(k_pe[:, :, None, :], pos)
    kv = (kv_lat @ w_kv_up).reshape(b, s, H_MLA, D_NOPE + D_V_MLA)
    k_nope, v_mla = jnp.split(kv, [D_NOPE], axis=-1)
    q_full = jnp.concatenate([q_nope, q_pe], axis=-1)
    k_full = jnp.concatenate([k_nope, jnp.broadcast_to(k_pe, (b, s, H_MLA, D_ROPE))], axis=-1)
    o_mla = _blocked_attention(q_full, k_full, v_mla, (D_NOPE + D_ROPE) ** -0.5)
    o_mla = o_mla.reshape(b, s, W_MLA) * jax.nn.silu(z_mla)

    mn = _rmsnorm(mem, mem_norm_gain)
    km, vm = jnp.split(mn @ w_mem_kv, 2, axis=-1)
    km = km.reshape(b, N_MEM, H_MEM, D_MEM)
    vm = vm.reshape(b, N_MEM, H_MEM, D_MEM)
    qx = q_mem.reshape(b, s, H_MEM, D_MEM)
    sc = jnp.einsum("bqhd,bkhd->bhqk", qx, km).astype(jnp.float32) * D_MEM ** -0.5
    p = jax.nn.softmax(sc, axis=-1)
    o_mem = jnp.einsum("bhqk,bkhd->bqhd", p.astype(vm.dtype), vm).reshape(b, s, W_MEM) * jax.nn.silu(z_mem)

    gates = jax.nn.sigmoid(gate_logits.astype(jnp.float32)).astype(x.dtype).reshape(b, s, N_BRANCH, D_MODEL)
    merged = (gates[:, :, 0] * (o_gdn @ w_br_gdn)
              + gates[:, :, 1] * (o_mla @ w_br_mla)
              + gates[:, :, 2] * (o_mem @ w_br_mem))
    return x + merged @ w_out


def _encode(x, mem, layer_weights, final_norm_gain):
    pos = jnp.arange(x.shape[1], dtype=jnp.int32)
    h = x
    for l in range(DEPTH):
        h = _layer(h, mem, pos, *[w[l] for w in layer_weights])
    return _rmsnorm(h, final_norm_gain)


def setup_inputs(seed: int = 0) -> dict:
    key = jax.random.key(seed)
    ks = jax.random.split(key, 24)
    f32 = jnp.float32

    def nrm(k, shape, fan_in):
        return jax.random.normal(k, shape, f32) * fan_in ** -0.5

    def gain(k, shape):
        return 1.0 + 0.02 * jax.random.normal(k, shape, f32)

    dt = jnp.exp(jax.random.uniform(ks[6], (DEPTH, 2, H_GDN), f32, math.log(1e-3), math.log(0.1)))
    return {
        "x_prompt": jax.random.normal(ks[0], (BATCH, SEQ, D_MODEL), f32),
        "x_sample": jax.random.normal(ks[1], (DEC_BATCH, DEC_SEQ, D_MODEL), f32),
        "mem_prompt": jax.random.normal(ks[2], (BATCH, N_MEM, D_MODEL), f32),
        "mem_sample": jax.random.normal(ks[3], (DEC_BATCH, N_MEM, D_MODEL), f32),
        "attn_norm_gain": gain(ks[4], (DEPTH, D_MODEL)),
        "w_in": nrm(ks[5], (DEPTH, D_MODEL, D_IN), D_MODEL),
        "conv_w": nrm(ks[7], (DEPTH, CONV_K, W_QKV), CONV_K),
        "a_log": jnp.log(jax.random.uniform(ks[8], (DEPTH, 2, H_GDN), f32, 1.0, 16.0)),
        "dt_bias": dt + jnp.log(-jnp.expm1(-dt)),
        "gdn_norm_gain": gain(ks[9], (DEPTH, DV_GDN)),
        "q_norm_gain": gain(ks[10], (DEPTH, Q_LORA)),
        "w_q_up": nrm(ks[11], (DEPTH, Q_LORA, H_MLA * (D_NOPE + D_ROPE)), Q_LORA),
        "kv_norm_gain": gain(ks[12], (DEPTH, KV_LORA)),
        "w_kv_up": nrm(ks[13], (DEPTH, KV_LORA, H_MLA * (D_NOPE + D_V_MLA)), KV_LORA),
        "mem_norm_gain": gain(ks[14], (DEPTH, D_MODEL)),
        "w_mem_kv": nrm(ks[15], (DEPTH, D_MODEL, 2 * W_MEM), D_MODEL),
        "w_br_gdn": nrm(ks[16], (DEPTH, W_GDN, D_MODEL), W_GDN),
        "w_br_mla": nrm(ks[17], (DEPTH, W_MLA, D_MODEL), W_MLA),
        "w_br_mem": nrm(ks[18], (DEPTH, W_MEM, D_MODEL), W_MEM),
        "w_out": nrm(ks[19], (DEPTH, D_MODEL, D_MODEL), D_MODEL),
        "final_norm_gain": gain(ks[20], (D_MODEL,)),
    }


def reference(x_prompt, x_sample, mem_prompt, mem_sample, attn_norm_gain, w_in, conv_w, a_log,
              dt_bias, gdn_norm_gain, q_norm_gain, w_q_up, kv_norm_gain, w_kv_up, mem_norm_gain,
              w_mem_kv, w_br_gdn, w_br_mla, w_br_mem, w_out, final_norm_gain):
    layer_weights = (attn_norm_gain, w_in, conv_w, a_log, dt_bias, gdn_norm_gain, q_norm_gain,
                     w_q_up, kv_norm_gain, w_kv_up, mem_norm_gain, w_mem_kv, w_br_gdn,
                     w_br_mla, w_br_mem, w_out)
    y_prompt = _encode(x_prompt, mem_prompt, layer_weights, final_norm_gain)
    y_sample = _encode(x_sample, mem_sample, layer_weights, final_norm_gain)
    return (y_prompt, y_sample)
```

```python
import functools
import math

import numpy as np
import jax
import jax.numpy as jnp
from jax import lax
from jax.experimental import pallas as pl
from jax.experimental.pallas import tpu as pltpu

D_MODEL = 1024
H_GDN = 8
DK_GDN = 128
DV_GDN = 128
W_GDN = H_GDN * DV_GDN
CONV_K = 5
CHUNK = 64
H_MLA = 8
Q_LORA = 384
KV_LORA = 256
D_NOPE = 128
D_ROPE = 64
D_V_MLA = 128
W_MLA = H_MLA * D_V_MLA
ROPE_THETA = 10000.0
H_MEM = 4
D_MEM = 128
W_MEM = H_MEM * D_MEM
N_BRANCH = 3
EPS = 1e-6
W_QKV = H_GDN * (2 * DK_GDN + DV_GDN)
SPLIT_SIZES = (W_QKV, 4 * H_GDN, W_GDN, Q_LORA, KV_LORA + D_ROPE, W_MLA, W_MEM, W_MEM, N_BRANCH * D_MODEL)

LANES = 128
SUBLANES = 8
D_QK_PAD = 256
VMEM_LIMIT = 56 * 1024 * 1024

F32 = jnp.float32
BF16 = jnp.bfloat16

MK_INCL, MK_STRICT, MK_PAIR, MK_MERGE0 = 0, 1, 2, 3
MERGE_SIZES = (2, 4, 8, 16, 32)
MK_PER_DIR = MK_MERGE0 + len(MERGE_SIZES)
MK_EYE = 2 * MK_PER_DIR


def _rms(x, gain):
    return x * lax.rsqrt(jnp.mean(x * x, axis=-1, keepdims=True) + EPS) * gain


def _sigmoid(x):
    return 1.0 / (1.0 + jnp.exp(-x))


def _silu(x):
    return x * _sigmoid(x)


def _dot(a, b):
    return jnp.dot(a, b, preferred_element_type=F32)


def _dot_nt(a, b):
    return lax.dot_general(a, b, (((1,), (1,)), ((), ())), preferred_element_type=F32)


def _dot_tn(a, b):
    return lax.dot_general(a, b, (((0,), (0,)), ((), ())), preferred_element_type=F32)


def _proj_kernel(x_ref, gain_ref, w_ref, o_ref):
    h = _rms(x_ref[...], gain_ref[...]).astype(BF16)
    o_ref[...] = _dot(h, w_ref[...]).astype(o_ref.dtype)


def _proj(x2, gain, w2, tm):
    t = x2.shape[0]
    ng, _, n = w2.shape
    return pl.pallas_call(
        _proj_kernel,
        grid=(ng, t // tm),
        in_specs=[
            pl.BlockSpec((tm, D_MODEL), lambda j, i: (i, 0)),
            pl.BlockSpec((1, D_MODEL), lambda j, i: (0, 0)),
            pl.BlockSpec((None, D_MODEL, n), lambda j, i: (j, 0, 0)),
        ],
        out_specs=pl.BlockSpec((None, tm, n), lambda j, i: (j, i, 0)),
        out_shape=jax.ShapeDtypeStruct((ng, t, n), BF16),
        compiler_params=pltpu.CompilerParams(
            dimension_semantics=("arbitrary", "arbitrary"), vmem_limit_bytes=VMEM_LIMIT),
        name="proj",
    )(x2, gain, w2)


W_SMALL = Q_LORA + KV_LORA + LANES + LANES


def _mla_prep_kernel(x_ref, gain_ref, w_ref, qg_ref, kvg_ref, wq_ref, wkv_ref, tabq_ref, tabk_ref,
                     alog_ref, dtb_ref, qm_ref, kf_ref, v_ref, gb_ref):
    h = _rms(x_ref[...], gain_ref[...]).astype(BF16)
    p = _dot(h, w_ref[...])
    cq = p[:, :Q_LORA]
    kvl = p[:, Q_LORA:Q_LORA + KV_LORA]
    kpe = p[:, Q_LORA + KV_LORA:Q_LORA + KV_LORA + LANES]
    ab = p[:, Q_LORA + KV_LORA + LANES:]
    scale = (D_NOPE + D_ROPE) ** -0.5

    qm = _dot(_rms(cq, qg_ref[...]).astype(BF16), wq_ref[...])
    tabq = tabq_ref[...]
    for hh in range(H_MLA):
        c0 = hh * D_QK_PAD
        qm_ref[:, c0:c0 + D_NOPE] = (qm[:, c0:c0 + D_NOPE] * scale).astype(BF16)
        t = qm[:, c0 + D_NOPE:c0 + D_QK_PAD] * tabq
        qm_ref[:, c0 + D_NOPE:c0 + D_QK_PAD] = (t + pltpu.roll(t, D_ROPE, 1)).astype(BF16)

    kv = _dot(_rms(kvl, kvg_ref[...]).astype(BF16), wkv_ref[...])
    t = kpe * tabk_ref[...]
    t = t + pltpu.roll(t, D_ROPE, 1)
    lane = lax.broadcasted_iota(jnp.int32, t.shape, 1)
    kpe_rot = jnp.where(lane < D_ROPE, t, 0.0).astype(BF16)
    for hh in range(H_MLA):
        c0 = hh * D_QK_PAD
        kf_ref[:, c0:c0 + D_NOPE] = kv[:, hh * D_NOPE:(hh + 1) * D_NOPE].astype(BF16)
        kf_ref[:, c0 + D_NOPE:c0 + D_QK_PAD] = kpe_rot
    v_ref[...] = kv[:, W_MLA:].astype(BF16)

    a = ab + dtb_ref[...]
    softplus = jnp.maximum(a, 0.0) + jnp.log(1.0 + jnp.exp(-jnp.abs(a)))
    g = -jnp.exp(alog_ref[...]) * softplus
    beta = _sigmoid(ab)
    gb_ref[...] = jnp.where(lane < 2 * H_GDN, g, jnp.where(lane < 4 * H_GDN, beta, 0.0))


def _mla_prep(x2, gain, w_small, qg, kvg, wq, wkv, tabq, tabk, alog, dtb, tm, s):
    t = x2.shape[0]
    npos = s // tm
    const = lambda i: (0, 0)
    row = lambda i: (i, 0)
    return pl.pallas_call(
        _mla_prep_kernel,
        grid=(t // tm,),
        in_specs=[
            pl.BlockSpec((tm, D_MODEL), row),
            pl.BlockSpec((1, D_MODEL), const),
            pl.BlockSpec((D_MODEL, W_SMALL), const),
            pl.BlockSpec((1, Q_LORA), const),
            pl.BlockSpec((1, KV_LORA), const),
            pl.BlockSpec((Q_LORA, H_MLA * D_QK_PAD), const),
            pl.BlockSpec((KV_LORA, 2 * W_MLA), const),
            pl.BlockSpec((tm, LANES), lambda i: (i % npos, 0)),
            pl.BlockSpec((tm, LANES), lambda i: (i % npos, 0)),
            pl.BlockSpec((1, LANES), const),
            pl.BlockSpec((1, LANES), const),
        ],
        out_specs=[
            pl.BlockSpec((tm, H_MLA * D_QK_PAD), row),
            pl.BlockSpec((tm, H_MLA * D_QK_PAD), row),
            pl.BlockSpec((tm, W_MLA), row),
            pl.BlockSpec((tm, LANES), row),
        ],
        out_shape=[
            jax.ShapeDtypeStruct((t, H_MLA * D_QK_PAD), BF16),
            jax.ShapeDtypeStruct((t, H_MLA * D_QK_PAD), BF16),
            jax.ShapeDtypeStruct((t, W_MLA), BF16),
            jax.ShapeDtypeStruct((t, LANES), F32),
        ],
        compiler_params=pltpu.CompilerParams(
            dimension_semantics=("arbitrary",), vmem_limit_bytes=VMEM_LIMIT),
        name="mla_prep",
    )(x2, gain, w_small, qg, kvg, wq, wkv, tabq, tabk, alog, dtb)


def _mla_attn_kernel(q_ref, k_ref, v_ref, z_ref, o_ref):
    s = _dot_nt(q_ref[...], k_ref[...])
    p = jnp.exp(s - jnp.max(s, axis=-1, keepdims=True))
    l = jnp.sum(p, axis=-1, keepdims=True)
    o = _dot(p.astype(BF16), v_ref[...]) / l
    o_ref[...] = (o * _silu(z_ref[...].astype(F32))).astype(o_ref.dtype)


def _mla_attn(qm, kf, v, zq, b, s, tq):
    t = b * s
    nq = s // tq
    return pl.pallas_call(
        _mla_attn_kernel,
        grid=(b, H_MLA, nq),
        in_specs=[
            pl.BlockSpec((tq, D_QK_PAD), lambda bi, h, i: (bi * nq + i, h)),
            pl.BlockSpec((s, D_QK_PAD), lambda bi, h, i: (bi, h)),
            pl.BlockSpec((s, D_V_MLA), lambda bi, h, i: (bi, h)),
            pl.BlockSpec((None, tq, LANES), lambda bi, h, i: (1, bi * nq + i, W_GDN // LANES + h)),
        ],
        out_specs=pl.BlockSpec((tq, D_V_MLA), lambda bi, h, i: (bi * nq + i, h)),
        out_shape=jax.ShapeDtypeStruct((t, W_MLA), BF16),
        compiler_params=pltpu.CompilerParams(
            dimension_semantics=("arbitrary", "arbitrary", "arbitrary"), vmem_limit_bytes=VMEM_LIMIT),
        name="mla_attn",
    )(qm, kf, v, zq)


def _gdn_masks():
    i = np.arange(CHUNK)[:, None]
    j = np.arange(CHUNK)[None, :]
    out = []
    for d in range(2):
        before = (i > j) if d == 0 else (i < j)
        out.append((i == j) | before)
        out.append(before)
        out.append(before & (i // 2 == j // 2))
        for sz in MERGE_SIZES:
            out.append(before & (i // (2 * sz) == j // (2 * sz)) & (i // sz != j // sz))
    out.append(i == j)
    return np.stack(out).astype(np.float32)


def _gdn_kernel(q_ref, k_ref, v_ref, z_ref, gb_ref, cwq_ref, cwk_ref, cwv_ref, gain_ref, mk_ref, o_ref,
                xpad, qs, ks, vs, gc_s, bt_s, u_s, wq_s, kd_s, in_s, eg_s, o_s):
    head = pl.program_id(1)
    s = q_ref.shape[0]
    nc = s // CHUNK
    pad = SUBLANES
    halo = (CONV_K - 1) // 2

    def rows(c):
        return pl.ds(pl.multiple_of(c * CHUNK, CHUNK), CHUNK)

    zeros = jnp.zeros((pad, LANES), F32)
    xpad[0:pad, :] = zeros
    xpad[pad + s:2 * pad + s, :] = zeros
    for src, cw_ref, dst, kind in ((q_ref, cwq_ref, qs, "q"), (k_ref, cwk_ref, ks, "k"), (v_ref, cwv_ref, vs, "v")):
        xpad[pad:pad + s, :] = src[...].astype(F32)
        cw = cw_ref[...]

        def conv_body(c, carry, cw=cw, dst=dst, kind=kind):
            base = pl.multiple_of(c * CHUNK, CHUNK)
            win = xpad[pl.ds(base, CHUNK + 2 * pad), :]
            y = cw[0:1, :] * win[pad - halo:pad - halo + CHUNK]
            for j in range(1, CONV_K):
                y = y + cw[j:j + 1, :] * win[pad - halo + j:pad - halo + j + CHUNK]
            y = _silu(y)
            if kind != "v":
                y = y * lax.rsqrt(jnp.sum(y * y, axis=-1, keepdims=True) + EPS)
            if kind == "q":
                y = y * DK_GDN ** -0.5
            dst[pl.ds(base, CHUNK), :] = y
            return carry

        lax.fori_loop(0, nc, conv_body, 0)

    gbv = gb_ref[...]
    lane = lax.broadcasted_iota(jnp.int32, gbv.shape, 1)

    def column(cidx):
        col = jnp.sum(jnp.where(lane == cidx, gbv, 0.0), axis=-1, keepdims=True)
        return jnp.broadcast_to(col, gbv.shape)

    for d in range(2):
        gc_s[d] = column(d * H_GDN + head)
        bt_s[d] = column((2 + d) * H_GDN + head)

    rid = lax.broadcasted_iota(jnp.int32, (CHUNK, LANES), 0)

    def cumsum_body(c, carry):
        r = rows(c)
        y = gc_s[0, r, :]
        sh = 1
        while sh < CHUNK:
            y = y + jnp.where(rid >= sh, pltpu.roll(y, sh, 0), 0.0)
            sh *= 2
        gc_s[0, r, :] = y
        y = gc_s[1, r, :]
        sh = 1
        while sh < CHUNK:
            y = y + jnp.where(rid < CHUNK - sh, pltpu.roll(y, CHUNK - sh, 0), 0.0)
            sh *= 2
        gc_s[1, r, :] = y
        return carry

    lax.fori_loop(0, nc, cumsum_body, 0)

    eye = mk_ref[MK_EYE]

    def prep_body(c, carry):
        r = rows(c)
        q = qs[r, :]
        k = ks[r, :]
        v = vs[r, :]
        kb16 = k.astype(BF16)
        qk = _dot_nt(q.astype(BF16), kb16)
        for d in range(2):
            m0 = d * MK_PER_DIR
            gc = gc_s[d, r, :]
            bt = bt_s[d, r, :]
            diff = gc[:, :CHUNK] - jnp.transpose(gc)[:CHUNK, :]
            e = jnp.exp(jnp.where(mk_ref[m0 + MK_INCL] > 0.5, diff, -jnp.inf))
            kb = k * bt
            lm = _dot_nt(kb.astype(BF16), kb16) * e * mk_ref[m0 + MK_STRICT]
            x = eye - lm * mk_ref[m0 + MK_PAIR]
            for t in range(len(MERGE_SIZES)):
                y = _dot((lm * mk_ref[m0 + MK_MERGE0 + t]).astype(BF16), x.astype(BF16))
                x = x - _dot(x.astype(BF16), y.astype(BF16))
            gl = gc[CHUNK - 1:CHUNK, :] if d == 0 else gc[0:1, :]
            eg = jnp.exp(gc)
            rhs = jnp.concatenate([(v * bt).astype(BF16), (kb * eg).astype(BF16)], axis=1)
            uw = _dot(x.astype(BF16), rhs)
            u_s[d, c] = uw[:, :DV_GDN]
            wq_s[d, c, 0:CHUNK, :] = uw[:, DV_GDN:].astype(BF16)
            wq_s[d, c, CHUNK:2 * CHUNK, :] = (q * eg).astype(BF16)
            kd_s[d, c] = (k * jnp.exp(gl - gc)).astype(BF16)
            in_s[d, c] = (qk * e).astype(BF16)
            eg_s[d, c] = jnp.broadcast_to(jnp.exp(gl), (SUBLANES, LANES))
        return carry

    lax.fori_loop(0, nc, prep_body, 0)

    def scan_body(i, states):
        new_states = []
        for d in range(2):
            c = i if d == 0 else nc - 1 - i
            st = states[d]
            x1 = _dot(wq_s[d, c], st.astype(BF16))
            vn = (u_s[d, c] - x1[:CHUNK]).astype(BF16)
            o = x1[CHUNK:] + _dot(in_s[d, c], vn)
            o_s[d, rows(c), :] = o
            new_states.append(st * eg_s[d, c, 0:1, :] + _dot_tn(kd_s[d, c], vn))
        return tuple(new_states)

    st0 = jnp.zeros((DK_GDN, DV_GDN), F32)
    lax.fori_loop(0, nc, scan_body, (st0, st0))

    gain = gain_ref[...]

    def out_body(c, carry):
        r = rows(c)
        o = o_s[0, r, :] + o_s[1, r, :]
        o_ref[r, :] = (_rms(o, gain) * _silu(z_ref[r, :].astype(F32))).astype(o_ref.dtype)
        return carry

    lax.fori_loop(0, nc, out_body, 0)


def _gdn(pz, gb, conv_w, gdn_gain, masks, b, s):
    t = b * s
    nc = s // CHUNK
    col = lambda off: (lambda bi, h: (0, bi, off + h))
    cwcol = lambda off: (lambda bi, h: (0, off + h))
    return pl.pallas_call(
        _gdn_kernel,
        grid=(b, H_GDN),
        in_specs=[
            pl.BlockSpec((None, s, LANES), col(0)),
            pl.BlockSpec((None, s, LANES), col(H_GDN)),
            pl.BlockSpec((None, s, LANES), col(2 * H_GDN)),
            pl.BlockSpec((None, s, LANES), lambda bi, h: (1, bi, h)),
            pl.BlockSpec((s, LANES), lambda bi, h: (bi, 0)),
            pl.BlockSpec((CONV_K, LANES), cwcol(0)),
            pl.BlockSpec((CONV_K, LANES), cwcol(H_GDN)),
            pl.BlockSpec((CONV_K, LANES), cwcol(2 * H_GDN)),
            pl.BlockSpec((1, DV_GDN), lambda bi, h: (0, 0)),
            pl.BlockSpec(masks.shape, lambda bi, h: (0, 0, 0)),
        ],
        out_specs=pl.BlockSpec((s, LANES), lambda bi, h: (bi, h)),
        out_shape=jax.ShapeDtypeStruct((t, W_GDN), BF16),
        scratch_shapes=[
            pltpu.VMEM((s + 2 * SUBLANES, LANES), F32),
            pltpu.VMEM((s, LANES), F32),
            pltpu.VMEM((s, LANES), F32),
            pltpu.VMEM((s, LANES), F32),
            pltpu.VMEM((2, s, LANES), F32),
            pltpu.VMEM((2, s, LANES), F32),
            pltpu.VMEM((2, nc, CHUNK, DV_GDN), F32),
            pltpu.VMEM((2, nc, 2 * CHUNK, DK_GDN), BF16),
            pltpu.VMEM((2, nc, CHUNK, DK_GDN), BF16),
            pltpu.VMEM((2, nc, CHUNK, CHUNK), BF16),
            pltpu.VMEM((2, nc, SUBLANES, LANES), F32),
            pltpu.VMEM((2, s, LANES), F32),
        ],
        compiler_params=pltpu.CompilerParams(
            dimension_semantics=("arbitrary", "arbitrary"), vmem_limit_bytes=VMEM_LIMIT),
        name="gdn",
    )(pz, pz, pz, pz, gb, conv_w, conv_w, conv_w, gdn_gain, masks)


def _mem_kv_kernel(m_ref, gain_ref, w_ref, o_ref):
    o_ref[...] = _dot(_rms(m_ref[...], gain_ref[...]).astype(BF16), w_ref[...]).astype(o_ref.dtype)


def _mem_kv(mem, gain, w):
    b, n, _ = mem.shape
    return pl.pallas_call(
        _mem_kv_kernel,
        grid=(b,),
        in_specs=[
            pl.BlockSpec((None, n, D_MODEL), lambda bi: (bi, 0, 0)),
            pl.BlockSpec((1, D_MODEL), lambda bi: (0, 0)),
            pl.BlockSpec((D_MODEL, 2 * W_MEM), lambda bi: (0, 0)),
        ],
        out_specs=pl.BlockSpec((None, n, 2 * W_MEM), lambda bi: (bi, 0, 0)),
        out_shape=jax.ShapeDtypeStruct((b, n, 2 * W_MEM), BF16),
        compiler_params=pltpu.CompilerParams(
            dimension_semantics=("arbitrary",), vmem_limit_bytes=VMEM_LIMIT),
        name="mem_kv",
    )(mem, gain, w)


def _out_kernel(x_ref, again_ref, wg_ref, og_ref, om_ref, zq_ref, kvm_ref, wbg_ref, wbm_ref, wbe_ref, wo_ref,
                fgain_ref, o_ref):
    x = x_ref[...]
    h = _rms(x, again_ref[...]).astype(BF16)
    gates = _sigmoid(_dot(h, wg_ref[...]))

    zq = zq_ref[...]
    kvm = kvm_ref[...]
    heads = []
    for hh in range(H_MEM):
        c0 = hh * D_MEM
        sc = _dot_nt(zq[:, W_MEM + c0:W_MEM + c0 + D_MEM], kvm[:, c0:c0 + D_MEM]) * D_MEM ** -0.5
        p = jnp.exp(sc - jnp.max(sc, axis=-1, keepdims=True))
        l = jnp.sum(p, axis=-1, keepdims=True)
        o = _dot(p.astype(BF16), kvm[:, W_MEM + c0:W_MEM + c0 + D_MEM]) / l
        heads.append((o * _silu(zq[:, c0:c0 + D_MEM].astype(F32))).astype(BF16))
    o_mem = jnp.concatenate(heads, axis=1)

    merged = (gates[:, :D_MODEL] * _dot(og_ref[...], wbg_ref[...])
              + gates[:, D_MODEL:2 * D_MODEL] * _dot(om_ref[...], wbm_ref[...])
              + gates[:, 2 * D_MODEL:] * _dot(o_mem, wbe_ref[...]))
    y = x + _dot(merged.astype(BF16), wo_ref[...])
    o_ref[...] = _rms(y, fgain_ref[...])


def _out(x2, again, wg, og, om, pz, kvm, wbg, wbm, wbe, wo, fgain, s, tm):
    t = x2.shape[0]
    nt = s // tm
    const = lambda i: (0, 0)
    row = lambda i: (i, 0)
    return pl.pallas_call(
        _out_kernel,
        grid=(t // tm,),
        in_specs=[
            pl.BlockSpec((tm, D_MODEL), row),
            pl.BlockSpec((1, D_MODEL), const),
            pl.BlockSpec((D_MODEL, N_BRANCH * D_MODEL), const),
            pl.BlockSpec((tm, W_GDN), row),
            pl.BlockSpec((tm, W_MLA), row),
            pl.BlockSpec((None, tm, 2 * W_MEM), lambda i: (1, i, (W_GDN + W_MLA) // (2 * W_MEM))),
            pl.BlockSpec((None, kvm.shape[1], 2 * W_MEM), lambda i: (i // nt, 0, 0)),
            pl.BlockSpec((W_GDN, D_MODEL), const),
            pl.BlockSpec((W_MLA, D_MODEL), const),
            pl.BlockSpec((W_MEM, D_MODEL), const),
            pl.BlockSpec((D_MODEL, D_MODEL), const),
            pl.BlockSpec((1, D_MODEL), const),
        ],
        out_specs=pl.BlockSpec((tm, D_MODEL), row),
        out_shape=jax.ShapeDtypeStruct((t, D_MODEL), F32),
        compiler_params=pltpu.CompilerParams(
            dimension_semantics=("arbitrary",), vmem_limit_bytes=VMEM_LIMIT),
        name="out",
    )(x2, again, wg, og, om, pz, kvm, wbg, wbm, wbe, wo, fgain)


def _prep_weights(attn_norm_gain, w_in, conv_w, a_log, dt_bias, gdn_norm_gain, q_norm_gain, w_q_up,
                  kv_norm_gain, w_kv_up, mem_norm_gain, w_mem_kv, w_br_gdn, w_br_mla, w_br_mem, w_out,
                  final_norm_gain):
    w = w_in[0]
    offs = np.concatenate([[0], np.cumsum(SPLIT_SIZES)])
    seg = lambda n: w[:, int(offs[n]):int(offs[n + 1])]
    w_qkv, w_ab, w_zg, w_cq, w_ckv, w_zm, w_qmem, w_zmem, w_gate = (seg(n) for n in range(9))
    half = D_ROPE // 2

    def rope_cols(wpe):
        x1, x2 = wpe[..., :half], wpe[..., half:]
        return jnp.concatenate([x1, x2, x2, x1], axis=-1)

    w_big = jnp.stack([w_qkv, jnp.concatenate([w_zg, w_zm, w_zmem, w_qmem], axis=1)]).astype(BF16)
    w_small = jnp.concatenate([
        w_cq, w_ckv[:, :KV_LORA], rope_cols(w_ckv[:, KV_LORA:]),
        w_ab, jnp.zeros((D_MODEL, LANES - 4 * H_GDN), w.dtype)], axis=1).astype(BF16)

    wq = w_q_up[0].reshape(Q_LORA, H_MLA, D_NOPE + D_ROPE)
    wq = jnp.concatenate([wq[..., :D_NOPE], rope_cols(wq[..., D_NOPE:])], axis=-1)
    wq = wq.reshape(Q_LORA, H_MLA * D_QK_PAD).astype(BF16)
    wkv = w_kv_up[0].reshape(KV_LORA, H_MLA, D_NOPE + D_V_MLA)
    wkv = jnp.concatenate([wkv[..., :D_NOPE].reshape(KV_LORA, W_MLA),
                           wkv[..., D_NOPE:].reshape(KV_LORA, W_MLA)], axis=1).astype(BF16)

    pad_lanes = lambda v: jnp.concatenate([v.reshape(1, -1), jnp.zeros((1, LANES - v.size), F32)], axis=1)
    return dict(
        again=attn_norm_gain[0].reshape(1, D_MODEL), w_big=w_big, w_small=w_small, w_gate=w_gate.astype(BF16),
        qg=q_norm_gain[0].reshape(1, Q_LORA), kvg=kv_norm_gain[0].reshape(1, KV_LORA), wq=wq, wkv=wkv,
        alog=pad_lanes(a_log[0]), dtb=pad_lanes(dt_bias[0]),
        conv_w=conv_w[0], gdn_gain=gdn_norm_gain[0].reshape(1, DV_GDN),
        mgain=mem_norm_gain[0].reshape(1, D_MODEL), w_mem_kv=w_mem_kv[0].astype(BF16),
        wbg=w_br_gdn[0].astype(BF16), wbm=w_br_mla[0].astype(BF16), wbe=w_br_mem[0].astype(BF16),
        wo=w_out[0].astype(BF16), fgain=final_norm_gain.reshape(1, D_MODEL),
        masks=jnp.asarray(_gdn_masks()),
    )


def _rope_tables(s):
    half = D_ROPE // 2
    inv_freq = ROPE_THETA ** (-jnp.arange(half, dtype=F32) / half)
    ang = jnp.arange(s, dtype=jnp.int32).astype(F32)[:, None] * inv_freq[None, :]
    cos, sin = jnp.cos(ang), jnp.sin(ang)
    tabk = jnp.concatenate([cos, cos, -sin, sin], axis=1)
    return tabk * (D_NOPE + D_ROPE) ** -0.5, tabk


def _tile(s, want):
    return math.gcd(s, want)


def _encode(x, mem, wts):
    b, s, _ = x.shape
    x2 = x.reshape(b * s, D_MODEL)
    tabq, tabk = _rope_tables(s)
    pz = _proj(x2, wts["again"], wts["w_big"], _tile(s, 512))
    qm, kf, v, gb = _mla_prep(x2, wts["again"], wts["w_small"], wts["qg"], wts["kvg"], wts["wq"], wts["wkv"],
                              tabq, tabk, wts["alog"], wts["dtb"], _tile(s, 256), s)
    o_mla = _mla_attn(qm, kf, v, pz, b, s, _tile(s, 256))
    o_gdn = _gdn(pz, gb, wts["conv_w"], wts["gdn_gain"], wts["masks"], b, s)
    kvm = _mem_kv(mem, wts["mgain"], wts["w_mem_kv"])
    y = _out(x2, wts["again"], wts["w_gate"], o_gdn, o_mla, pz, kvm, wts["wbg"], wts["wbm"], wts["wbe"],
             wts["wo"], wts["fgain"], s, _tile(s, 256))
    return y.reshape(b, s, D_MODEL)


def kernel(x_prompt, x_sample, mem_prompt, mem_sample, attn_norm_gain, w_in, conv_w, a_log, dt_bias, gdn_norm_gain, q_norm_gain, w_q_up, kv_norm_gain, w_kv_up, mem_norm_gain, w_mem_kv, w_br_gdn, w_br_mla, w_br_mem, w_out, final_norm_gain):
    wts = _prep_weights(attn_norm_gain, w_in, conv_w, a_log, dt_bias, gdn_norm_gain, q_norm_gain, w_q_up,
                        kv_norm_gain, w_kv_up, mem_norm_gain, w_mem_kv, w_br_gdn, w_br_mla, w_br_mem, w_out,
                        final_norm_gain)
    return (_encode(x_prompt, mem_prompt, wts), _encode(x_sample, mem_sample, wts))
```

```python
import functools
import math

import numpy as np
import jax
import jax.numpy as jnp
from jax import lax
from jax.experimental import pallas as pl
from jax.experimental.pallas import tpu as pltpu

D_MODEL = 1024
H_GDN = 8
DK_GDN = 128
DV_GDN = 128
W_GDN = H_GDN * DV_GDN
CONV_K = 5
CHUNK = 64
H_MLA = 8
Q_LORA = 384
KV_LORA = 256
D_NOPE = 128
D_ROPE = 64
D_V_MLA = 128
W_MLA = H_MLA * D_V_MLA
ROPE_THETA = 10000.0
H_MEM = 4
D_MEM = 128
W_MEM = H_MEM * D_MEM
N_BRANCH = 3
EPS = 1e-6
W_QKV = H_GDN * (2 * DK_GDN + DV_GDN)
SPLIT_SIZES = (W_QKV, 4 * H_GDN, W_GDN, Q_LORA, KV_LORA + D_ROPE, W_MLA, W_MEM, W_MEM, N_BRANCH * D_MODEL)

LANES = 128
SUBLANES = 8
D_QK_PAD = 256
VMEM_LIMIT = 56 * 1024 * 1024

F32 = jnp.float32
BF16 = jnp.bfloat16

MK_INCL, MK_STRICT, MK_PAIR, MK_MERGE0 = 0, 1, 2, 3
MERGE_SIZES = (2, 4, 8, 16, 32)
MK_EYE = MK_MERGE0 + len(MERGE_SIZES)


def _rms(x, gain):
    return x * lax.rsqrt(jnp.mean(x * x, axis=-1, keepdims=True) + EPS) * gain


def _sigmoid(x):
    return 1.0 / (1.0 + jnp.exp(-x))


def _silu(x):
    return x * _sigmoid(x)


def _dot(a, b):
    return jnp.dot(a, b, preferred_element_type=F32)


def _dot_nt(a, b):
    return lax.dot_general(a, b, (((1,), (1,)), ((), ())), preferred_element_type=F32)


def _dot_tn(a, b):
    return lax.dot_general(a, b, (((0,), (0,)), ((), ())), preferred_element_type=F32)


def _proj_kernel(x_ref, gain_ref, w_ref, o_ref):
    h = _rms(x_ref[...], gain_ref[...]).astype(BF16)
    o_ref[...] = _dot(h, w_ref[...]).astype(o_ref.dtype)


def _proj(x2, gain, w2, tm):
    t = x2.shape[0]
    ng, _, n = w2.shape
    return pl.pallas_call(
        _proj_kernel,
        grid=(ng, t // tm),
        in_specs=[
            pl.BlockSpec((tm, D_MODEL), lambda j, i: (i, 0)),
            pl.BlockSpec((1, D_MODEL), lambda j, i: (0, 0)),
            pl.BlockSpec((None, D_MODEL, n), lambda j, i: (j, 0, 0)),
        ],
        out_specs=pl.BlockSpec((None, tm, n), lambda j, i: (j, i, 0)),
        out_shape=jax.ShapeDtypeStruct((ng, t, n), BF16),
        compiler_params=pltpu.CompilerParams(
            dimension_semantics=("arbitrary", "arbitrary"), vmem_limit_bytes=VMEM_LIMIT),
        name="proj",
    )(x2, gain, w2)


W_SMALL = Q_LORA + KV_LORA + LANES + LANES


def _mla_prep_kernel(x_ref, gain_ref, w_ref, qg_ref, kvg_ref, wq_ref, wkv_ref, tabq_ref, tabk_ref,
                     alog_ref, dtb_ref, qm_ref, kf_ref, v_ref, gb_ref):
    h = _rms(x_ref[...], gain_ref[...]).astype(BF16)
    p = _dot(h, w_ref[...])
    cq = p[:, :Q_LORA]
    kvl = p[:, Q_LORA:Q_LORA + KV_LORA]
    kpe = p[:, Q_LORA + KV_LORA:Q_LORA + KV_LORA + LANES]
    ab = p[:, Q_LORA + KV_LORA + LANES:]
    scale = (D_NOPE + D_ROPE) ** -0.5

    qm = _dot(_rms(cq, qg_ref[...]).astype(BF16), wq_ref[...])
    tabq = tabq_ref[...]
    for hh in range(H_MLA):
        c0 = hh * D_QK_PAD
        qm_ref[:, c0:c0 + D_NOPE] = (qm[:, c0:c0 + D_NOPE] * scale).astype(BF16)
        t = qm[:, c0 + D_NOPE:c0 + D_QK_PAD] * tabq
        qm_ref[:, c0 + D_NOPE:c0 + D_QK_PAD] = (t + pltpu.roll(t, D_ROPE, 1)).astype(BF16)

    kv = _dot(_rms(kvl, kvg_ref[...]).astype(BF16), wkv_ref[...])
    t = kpe * tabk_ref[...]
    t = t + pltpu.roll(t, D_ROPE, 1)
    lane = lax.broadcasted_iota(jnp.int32, t.shape, 1)
    kpe_rot = jnp.where(lane < D_ROPE, t, 0.0).astype(BF16)
    for hh in range(H_MLA):
        c0 = hh * D_QK_PAD
        kf_ref[:, c0:c0 + D_NOPE] = kv[:, hh * D_NOPE:(hh + 1) * D_NOPE].astype(BF16)
        kf_ref[:, c0 + D_NOPE:c0 + D_QK_PAD] = kpe_rot
    v_ref[...] = kv[:, W_MLA:].astype(BF16)

    a = ab + dtb_ref[...]
    softplus = jnp.maximum(a, 0.0) + jnp.log(1.0 + jnp.exp(-jnp.abs(a)))
    g = -jnp.exp(alog_ref[...]) * softplus
    beta = _sigmoid(ab)
    gb_ref[...] = jnp.where(lane < 2 * H_GDN, g, jnp.where(lane < 4 * H_GDN, beta, 0.0))


def _mla_prep(x2, gain, w_small, qg, kvg, wq, wkv, tabq, tabk, alog, dtb, tm, s):
    t = x2.shape[0]
    npos = s // tm
    const = lambda i: (0, 0)
    row = lambda i: (i, 0)
    return pl.pallas_call(
        _mla_prep_kernel,
        grid=(t // tm,),
        in_specs=[
            pl.BlockSpec((tm, D_MODEL), row),
            pl.BlockSpec((1, D_MODEL), const),
            pl.BlockSpec((D_MODEL, W_SMALL), const),
            pl.BlockSpec((1, Q_LORA), const),
            pl.BlockSpec((1, KV_LORA), const),
            pl.BlockSpec((Q_LORA, H_MLA * D_QK_PAD), const),
            pl.BlockSpec((KV_LORA, 2 * W_MLA), const),
            pl.BlockSpec((tm, LANES), lambda i: (i % npos, 0)),
            pl.BlockSpec((tm, LANES), lambda i: (i % npos, 0)),
            pl.BlockSpec((1, LANES), const),
            pl.BlockSpec((1, LANES), const),
        ],
        out_specs=[
            pl.BlockSpec((tm, H_MLA * D_QK_PAD), row),
            pl.BlockSpec((tm, H_MLA * D_QK_PAD), row),
            pl.BlockSpec((tm, W_MLA), row),
            pl.BlockSpec((tm, LANES), row),
        ],
        out_shape=[
            jax.ShapeDtypeStruct((t, H_MLA * D_QK_PAD), BF16),
            jax.ShapeDtypeStruct((t, H_MLA * D_QK_PAD), BF16),
            jax.ShapeDtypeStruct((t, W_MLA), BF16),
            jax.ShapeDtypeStruct((t, LANES), F32),
        ],
        compiler_params=pltpu.CompilerParams(
            dimension_semantics=("arbitrary",), vmem_limit_bytes=VMEM_LIMIT),
        name="mla_prep",
    )(x2, gain, w_small, qg, kvg, wq, wkv, tabq, tabk, alog, dtb)


def _mla_attn_kernel(q_ref, k_ref, v_ref, z_ref, o_ref):
    s = _dot_nt(q_ref[...], k_ref[...])
    p = jnp.exp(s - jnp.max(s, axis=-1, keepdims=True))
    l = jnp.sum(p, axis=-1, keepdims=True)
    o = _dot(p.astype(BF16), v_ref[...]) / l
    o_ref[...] = (o * _silu(z_ref[...].astype(F32))).astype(o_ref.dtype)


def _mla_attn(qm, kf, v, zq, b, s, tq):
    t = b * s
    nq = s // tq
    return pl.pallas_call(
        _mla_attn_kernel,
        grid=(b, H_MLA, nq),
        in_specs=[
            pl.BlockSpec((tq, D_QK_PAD), lambda bi, h, i: (bi * nq + i, h)),
            pl.BlockSpec((s, D_QK_PAD), lambda bi, h, i: (bi, h)),
            pl.BlockSpec((s, D_V_MLA), lambda bi, h, i: (bi, h)),
            pl.BlockSpec((None, tq, LANES), lambda bi, h, i: (1, bi * nq + i, W_GDN // LANES + h)),
        ],
        out_specs=pl.BlockSpec((tq, D_V_MLA), lambda bi, h, i: (bi * nq + i, h)),
        out_shape=jax.ShapeDtypeStruct((t, W_MLA), BF16),
        compiler_params=pltpu.CompilerParams(
            dimension_semantics=("arbitrary", "arbitrary", "arbitrary"), vmem_limit_bytes=VMEM_LIMIT),
        name="mla_attn",
    )(qm, kf, v, zq)


def _gdn_masks():
    i = np.arange(CHUNK)[:, None]
    j = np.arange(CHUNK)[None, :]
    per_dir = []
    for d in range(2):
        before = (i > j) if d == 0 else (i < j)
        out = [(i == j) | before,
               before,
               before & (i // 2 == j // 2)]
        for sz in MERGE_SIZES:
            out.append(before & (i // (2 * sz) == j // (2 * sz)) & (i // sz != j // sz))
        out.append(i == j)
        per_dir.append(np.stack(out))
    return np.concatenate(per_dir, axis=2).astype(np.float32)


GDN_HEADS_PER_STEP = 2
GDN_CHUNKS_PER_ITER = 8


def _gdn_kernel(q_ref, k_ref, v_ref, z_ref, gb_ref, cwq_ref, cwk_ref, cwv_ref, gain_ref, mk_ref, o_ref,
                xpad, qs, ks, vs, gc_s, bt_s, mp_s, n_s, r_s, eg_s, o_s):
    s = q_ref.shape[0]
    nc = s // CHUNK
    pad = SUBLANES
    halo = (CONV_K - 1) // 2
    group = math.gcd(nc, GDN_CHUNKS_PER_ITER)
    nheads = GDN_HEADS_PER_STEP

    def rows(c):
        return pl.ds(pl.multiple_of(c * CHUNK, CHUNK), CHUNK)

    def group_loop(body):
        def grouped(i, carry):
            body([i * group + t for t in range(group)])
            return carry
        lax.fori_loop(0, nc // group, grouped, 0)

    def chunk_loop(body):
        def each(cs):
            for c in cs:
                body(c)
        group_loop(each)

    zeros = jnp.zeros((pad, LANES), F32)
    xpad[0:pad, :] = zeros
    xpad[pad + s:2 * pad + s, :] = zeros
    gbv = gb_ref[...]
    lane = lax.broadcasted_iota(jnp.int32, gbv.shape, 1)
    rid = lax.broadcasted_iota(jnp.int32, (CHUNK, LANES), 0)
    fwd_lanes = lax.broadcasted_iota(jnp.int32, (CHUNK, LANES), 1) < CHUNK

    def block_diag(z):
        return jnp.concatenate([jnp.where(fwd_lanes, z, 0.0), jnp.where(fwd_lanes, 0.0, z)], axis=0).astype(BF16)

    for hh in range(nheads):
        head = pl.program_id(1) * nheads + hh
        cols = slice(hh * LANES, (hh + 1) * LANES)

        for src, cw_ref, dst, kind in ((q_ref, cwq_ref, qs, "q"), (k_ref, cwk_ref, ks, "k"),
                                       (v_ref, cwv_ref, vs, "v")):
            xpad[pad:pad + s, :] = src[:, cols].astype(F32)
            cw = cw_ref[:, cols]

            def conv_body(c, cw=cw, dst=dst, kind=kind):
                base = pl.multiple_of(c * CHUNK, CHUNK)
                win = xpad[pl.ds(base, CHUNK + 2 * pad), :]
                y = cw[0:1, :] * win[pad - halo:pad - halo + CHUNK]
                for j in range(1, CONV_K):
                    y = y + cw[j:j + 1, :] * win[pad - halo + j:pad - halo + j + CHUNK]
                y = _silu(y)
                if kind != "v":
                    y = y * lax.rsqrt(jnp.sum(y * y, axis=-1, keepdims=True) + EPS)
                if kind == "q":
                    y = y * DK_GDN ** -0.5
                dst[pl.ds(base, CHUNK), :] = y

            chunk_loop(conv_body)

        def column(cidx):
            col = jnp.sum(jnp.where(lane == cidx, gbv, 0.0), axis=-1, keepdims=True)
            return jnp.broadcast_to(col, gbv.shape)

        for d in range(2):
            gc_s[d] = column(d * H_GDN + head)
            bt_s[d] = column((2 + d) * H_GDN + head)

        def cumsum_body(c):
            r = rows(c)
            y = gc_s[0, r, :]
            sh = 1
            while sh < CHUNK:
                y = y + jnp.where(rid >= sh, pltpu.roll(y, sh, 0), 0.0)
                sh *= 2
            gc_s[0, r, :] = y
            y = gc_s[1, r, :]
            sh = 1
            while sh < CHUNK:
                y = y + jnp.where(rid < CHUNK - sh, pltpu.roll(y, CHUNK - sh, 0), 0.0)
                sh *= 2
            gc_s[1, r, :] = y

        chunk_loop(cumsum_body)

        def prep_group(cs, hh=hh):
            n = len(cs)
            rs = [rows(c) for c in cs]
            q = [qs[r, :] for r in rs]
            k = [ks[r, :] for r in rs]
            gc = [[gc_s[d, r, :] for d in range(2)] for r in rs]
            bt = [[bt_s[d, r, :] for d in range(2)] for r in rs]
            k16 = [t.astype(BF16) for t in k]
            qkk = [_dot_nt(jnp.concatenate([q[i].astype(BF16), k16[i]], axis=0),
                           jnp.concatenate([k16[i], k16[i]], axis=0)) for i in range(n)]
            e2, l2, x2 = [], [], []
            for i in range(n):
                gcol = jnp.where(fwd_lanes, gc[i][0], gc[i][1])
                grow = jnp.transpose(jnp.concatenate(gc[i], axis=0))[:CHUNK, :]
                e = jnp.exp(jnp.where(mk_ref[MK_INCL] > 0.5, gcol - grow, -jnp.inf))
                lm = qkk[i][CHUNK:] * jnp.where(fwd_lanes, bt[i][0], bt[i][1]) * e * mk_ref[MK_STRICT]
                e2.append(e)
                l2.append(lm)
                x2.append(mk_ref[MK_EYE] - lm * mk_ref[MK_PAIR])
            for t in range(len(MERGE_SIZES)):
                y2 = [_dot((l2[i] * mk_ref[MK_MERGE0 + t]).astype(BF16), block_diag(x2[i])) for i in range(n)]
                x2 = [x2[i] - _dot(x2[i].astype(BF16), block_diag(y2[i])) for i in range(n)]
            eg = [[jnp.exp(g) for g in gc[i]] for i in range(n)]
            gl = [[gc[i][0][CHUNK - 1:CHUNK, :], gc[i][1][0:1, :]] for i in range(n)]
            wu = []
            for i in range(n):
                v = vs[rs[i], :]
                rhs = jnp.concatenate(
                    [jnp.concatenate([(k[i] * bt[i][d] * eg[i][d]).astype(BF16), (v * bt[i][d]).astype(BF16)], axis=1)
                     for d in range(2)], axis=0)
                wu.append(_dot(block_diag(x2[i]), rhs).astype(BF16))
            iwu = [_dot(block_diag(qkk[i][:CHUNK] * e2[i]), wu[i]) for i in range(n)]
            kwu = [[_dot_tn((k[i] * jnp.exp(gl[i][d] - gc[i][d])).astype(BF16), wu[i][d * CHUNK:(d + 1) * CHUNK])
                    for d in range(2)] for i in range(n)]
            for i, c in enumerate(cs):
                for d in range(2):
                    iw = iwu[i][d * CHUNK:(d + 1) * CHUNK]
                    mp_s[hh, d, c, 0:DK_GDN, :] = (-kwu[i][d][:, :DK_GDN]).astype(BF16)
                    mp_s[hh, d, c, DK_GDN:DK_GDN + CHUNK, :] = (q[i] * eg[i][d] - iw[:, :DK_GDN]).astype(BF16)
                    n_s[hh, d, c] = kwu[i][d][:, DK_GDN:]
                    r_s[hh, d, c] = iw[:, DK_GDN:]
                    eg_s[hh, d, c] = jnp.broadcast_to(jnp.exp(gl[i][d]), (SUBLANES, LANES))

        group_loop(prep_group)

    def scan_body(i, states):
        new_states = []
        for hh in range(nheads):
            for d in range(2):
                c = i if d == 0 else nc - 1 - i
                st = states[2 * hh + d]
                x = _dot(mp_s[hh, d, c], st.astype(BF16))
                o_s[hh, d, rows(c), :] = x[DK_GDN:] + r_s[hh, d, c]
                new_states.append(st * eg_s[hh, d, c, 0:1, :] + x[:DK_GDN] + n_s[hh, d, c])
        return tuple(new_states)

    st0 = jnp.zeros((DK_GDN, DV_GDN), F32)
    lax.fori_loop(0, nc, scan_body, (st0,) * (2 * nheads))

    gain = gain_ref[...]
    for hh in range(nheads):
        cols = slice(hh * LANES, (hh + 1) * LANES)

        def out_body(c, hh=hh, cols=cols):
            r = rows(c)
            o = o_s[hh, 0, r, :] + o_s[hh, 1, r, :]
            o_ref[r, cols] = (_rms(o, gain) * _silu(z_ref[r, cols].astype(F32))).astype(o_ref.dtype)

        chunk_loop(out_body)


def _gdn(pz, gb, conv_w, gdn_gain, masks, b, s):
    t = b * s
    nc = s // CHUNK
    nh = GDN_HEADS_PER_STEP
    w = nh * LANES
    ng = H_GDN // nh
    col = lambda off: (lambda bi, h: (0, bi, off + h))
    cwcol = lambda off: (lambda bi, h: (0, off + h))
    return pl.pallas_call(
        _gdn_kernel,
        grid=(b, ng),
        in_specs=[
            pl.BlockSpec((None, s, w), col(0)),
            pl.BlockSpec((None, s, w), col(ng)),
            pl.BlockSpec((None, s, w), col(2 * ng)),
            pl.BlockSpec((None, s, w), lambda bi, h: (1, bi, h)),
            pl.BlockSpec((s, LANES), lambda bi, h: (bi, 0)),
            pl.BlockSpec((CONV_K, w), cwcol(0)),
            pl.BlockSpec((CONV_K, w), cwcol(ng)),
            pl.BlockSpec((CONV_K, w), cwcol(2 * ng)),
            pl.BlockSpec((1, DV_GDN), lambda bi, h: (0, 0)),
            pl.BlockSpec(masks.shape, lambda bi, h: (0, 0, 0)),
        ],
        out_specs=pl.BlockSpec((s, w), lambda bi, h: (bi, h)),
        out_shape=jax.ShapeDtypeStruct((t, W_GDN), BF16),
        scratch_shapes=[
            pltpu.VMEM((s + 2 * SUBLANES, LANES), F32),
            pltpu.VMEM((s, LANES), F32),
            pltpu.VMEM((s, LANES), F32),
            pltpu.VMEM((s, LANES), F32),
            pltpu.VMEM((2, s, LANES), F32),
            pltpu.VMEM((2, s, LANES), F32),
            pltpu.VMEM((nh, 2, nc, DK_GDN + CHUNK, DV_GDN), BF16),
            pltpu.VMEM((nh, 2, nc, DK_GDN, DV_GDN), F32),
            pltpu.VMEM((nh, 2, nc, CHUNK, DV_GDN), F32),
            pltpu.VMEM((nh, 2, nc, SUBLANES, LANES), F32),
            pltpu.VMEM((nh, 2, s, LANES), F32),
        ],
        compiler_params=pltpu.CompilerParams(
            dimension_semantics=("arbitrary", "arbitrary"), vmem_limit_bytes=VMEM_LIMIT),
        name="gdn",
    )(pz, pz, pz, pz, gb, conv_w, conv_w, conv_w, gdn_gain, masks)


def _mem_kv_kernel(m_ref, gain_ref, w_ref, o_ref):
    o_ref[...] = _dot(_rms(m_ref[...], gain_ref[...]).astype(BF16), w_ref[...]).astype(o_ref.dtype)


def _mem_kv(mem, gain, w):
    b, n, _ = mem.shape
    return pl.pallas_call(
        _mem_kv_kernel,
        grid=(b,),
        in_specs=[
            pl.BlockSpec((None, n, D_MODEL), lambda bi: (bi, 0, 0)),
            pl.BlockSpec((1, D_MODEL), lambda bi: (0, 0)),
            pl.BlockSpec((D_MODEL, 2 * W_MEM), lambda bi: (0, 0)),
        ],
        out_specs=pl.BlockSpec((None, n, 2 * W_MEM), lambda bi: (bi, 0, 0)),
        out_shape=jax.ShapeDtypeStruct((b, n, 2 * W_MEM), BF16),
        compiler_params=pltpu.CompilerParams(
            dimension_semantics=("arbitrary",), vmem_limit_bytes=VMEM_LIMIT),
        name="mem_kv",
    )(mem, gain, w)


def _out_kernel(x_ref, again_ref, wg_ref, og_ref, om_ref, zq_ref, kvm_ref, wbg_ref, wbm_ref, wbe_ref, wo_ref,
                fgain_ref, o_ref):
    x = x_ref[...]
    h = _rms(x, again_ref[...]).astype(BF16)
    gates = _sigmoid(_dot(h, wg_ref[...]))

    zq = zq_ref[...]
    kvm = kvm_ref[...]
    heads = []
    for hh in range(H_MEM):
        c0 = hh * D_MEM
        sc = _dot_nt(zq[:, W_MEM + c0:W_MEM + c0 + D_MEM], kvm[:, c0:c0 + D_MEM]) * D_MEM ** -0.5
        p = jnp.exp(sc - jnp.max(sc, axis=-1, keepdims=True))
        l = jnp.sum(p, axis=-1, keepdims=True)
        o = _dot(p.astype(BF16), kvm[:, W_MEM + c0:W_MEM + c0 + D_MEM]) / l
        heads.append((o * _silu(zq[:, c0:c0 + D_MEM].astype(F32))).astype(BF16))
    o_mem = jnp.concatenate(heads, axis=1)

    merged = (gates[:, :D_MODEL] * _dot(og_ref[...], wbg_ref[...])
              + gates[:, D_MODEL:2 * D_MODEL] * _dot(om_ref[...], wbm_ref[...])
              + gates[:, 2 * D_MODEL:] * _dot(o_mem, wbe_ref[...]))
    y = x + _dot(merged.astype(BF16), wo_ref[...])
    o_ref[...] = _rms(y, fgain_ref[...])


def _out(x2, again, wg, og, om, pz, kvm, wbg, wbm, wbe, wo, fgain, s, tm):
    t = x2.shape[0]
    nt = s // tm
    const = lambda i: (0, 0)
    row = lambda i: (i, 0)
    return pl.pallas_call(
        _out_kernel,
        grid=(t // tm,),
        in_specs=[
            pl.BlockSpec((tm, D_MODEL), row),
            pl.BlockSpec((1, D_MODEL), const),
            pl.BlockSpec((D_MODEL, N_BRANCH * D_MODEL), const),
            pl.BlockSpec((tm, W_GDN), row),
            pl.BlockSpec((tm, W_MLA), row),
            pl.BlockSpec((None, tm, 2 * W_MEM), lambda i: (1, i, (W_GDN + W_MLA) // (2 * W_MEM))),
            pl.BlockSpec((None, kvm.shape[1], 2 * W_MEM), lambda i: (i // nt, 0, 0)),
            pl.BlockSpec((W_GDN, D_MODEL), const),
            pl.BlockSpec((W_MLA, D_MODEL), const),
            pl.BlockSpec((W_MEM, D_MODEL), const),
            pl.BlockSpec((D_MODEL, D_MODEL), const),
            pl.BlockSpec((1, D_MODEL), const),
        ],
        out_specs=pl.BlockSpec((tm, D_MODEL), row),
        out_shape=jax.ShapeDtypeStruct((t, D_MODEL), F32),
        compiler_params=pltpu.CompilerParams(
            dimension_semantics=("arbitrary",), vmem_limit_bytes=VMEM_LIMIT),
        name="out",
    )(x2, again, wg, og, om, pz, kvm, wbg, wbm, wbe, wo, fgain)


def _prep_weights(attn_norm_gain, w_in, conv_w, a_log, dt_bias, gdn_norm_gain, q_norm_gain, w_q_up,
                  kv_norm_gain, w_kv_up, mem_norm_gain, w_mem_kv, w_br_gdn, w_br_mla, w_br_mem, w_out,
                  final_norm_gain):
    w = w_in[0]
    offs = np.concatenate([[0], np.cumsum(SPLIT_SIZES)])
    seg = lambda n: w[:, int(offs[n]):int(offs[n + 1])]
    w_qkv, w_ab, w_zg, w_cq, w_ckv, w_zm, w_qmem, w_zmem, w_gate = (seg(n) for n in range(9))
    half = D_ROPE // 2

    def rope_cols(wpe):
        x1, x2 = wpe[..., :half], wpe[..., half:]
        return jnp.concatenate([x1, x2, x2, x1], axis=-1)

    w_big = jnp.stack([w_qkv, jnp.concatenate([w_zg, w_zm, w_zmem, w_qmem], axis=1)]).astype(BF16)
    w_small = jnp.concatenate([
        w_cq, w_ckv[:, :KV_LORA], rope_cols(w_ckv[:, KV_LORA:]),
        w_ab, jnp.zeros((D_MODEL, LANES - 4 * H_GDN), w.dtype)], axis=1).astype(BF16)

    wq = w_q_up[0].reshape(Q_LORA, H_MLA, D_NOPE + D_ROPE)
    wq = jnp.concatenate([wq[..., :D_NOPE], rope_cols(wq[..., D_NOPE:])], axis=-1)
    wq = wq.reshape(Q_LORA, H_MLA * D_QK_PAD).astype(BF16)
    wkv = w_kv_up[0].reshape(KV_LORA, H_MLA, D_NOPE + D_V_MLA)
    wkv = jnp.concatenate([wkv[..., :D_NOPE].reshape(KV_LORA, W_MLA),
                           wkv[..., D_NOPE:].reshape(KV_LORA, W_MLA)], axis=1).astype(BF16)

    pad_lanes = lambda v: jnp.concatenate([v.reshape(1, -1), jnp.zeros((1, LANES - v.size), F32)], axis=1)
    return dict(
        again=attn_norm_gain[0].reshape(1, D_MODEL), w_big=w_big, w_small=w_small, w_gate=w_gate.astype(BF16),
        qg=q_norm_gain[0].reshape(1, Q_LORA), kvg=kv_norm_gain[0].reshape(1, KV_LORA), wq=wq, wkv=wkv,
        alog=pad_lanes(a_log[0]), dtb=pad_lanes(dt_bias[0]),
        conv_w=conv_w[0], gdn_gain=gdn_norm_gain[0].reshape(1, DV_GDN),
        mgain=mem_norm_gain[0].reshape(1, D_MODEL), w_mem_kv=w_mem_kv[0].astype(BF16),
        wbg=w_br_gdn[0].astype(BF16), wbm=w_br_mla[0].astype(BF16), wbe=w_br_mem[0].astype(BF16),
        wo=w_out[0].astype(BF16), fgain=final_norm_gain.reshape(1, D_MODEL),
        masks=jnp.asarray(_gdn_masks()),
    )


def _rope_tables(s):
    half = D_ROPE // 2
    inv_freq = ROPE_THETA ** (-jnp.arange(half, dtype=F32) / half)
    ang = jnp.arange(s, dtype=jnp.int32).astype(F32)[:, None] * inv_freq[None, :]
    cos, sin = jnp.cos(ang), jnp.sin(ang)
    tabk = jnp.concatenate([cos, cos, -sin, sin], axis=1)
    return tabk * (D_NOPE + D_ROPE) ** -0.5, tabk


def _tile(s, want):
    return math.gcd(s, want)


def _encode(x, mem, wts):
    b, s, _ = x.shape
    x2 = x.reshape(b * s, D_MODEL)
    tabq, tabk = _rope_tables(s)
    pz = _proj(x2, wts["again"], wts["w_big"], _tile(s, 512))
    qm, kf, v, gb = _mla_prep(x2, wts["again"], wts["w_small"], wts["qg"], wts["kvg"], wts["wq"], wts["wkv"],
                              tabq, tabk, wts["alog"], wts["dtb"], _tile(s, 256), s)
    o_mla = _mla_attn(qm, kf, v, pz, b, s, _tile(s, 256))
    o_gdn = _gdn(pz, gb, wts["conv_w"], wts["gdn_gain"], wts["masks"], b, s)
    kvm = _mem_kv(mem, wts["mgain"], wts["w_mem_kv"])
    y = _out(x2, wts["again"], wts["w_gate"], o_gdn, o_mla, pz, kvm, wts["wbg"], wts["wbm"], wts["wbe"],
             wts["wo"], wts["fgain"], s, _tile(s, 256))
    return y.reshape(b, s, D_MODEL)


def kernel(x_prompt, x_sample, mem_prompt, mem_sample, attn_norm_gain, w_in, conv_w, a_log, dt_bias, gdn_norm_gain, q_norm_gain, w_q_up, kv_norm_gain, w_kv_up, mem_norm_gain, w_mem_kv, w_br_gdn, w_br_mla, w_br_mem, w_out, final_norm_gain):
    wts = _prep_weights(attn_norm_gain, w_in, conv_w, a_log, dt_bias, gdn_norm_gain, q_norm_gain, w_q_up,
                        kv_norm_gain, w_kv_up, mem_norm_gain, w_mem_kv, w_br_gdn, w_br_mla, w_br_mem, w_out,
                        final_norm_gain)
    return (_encode(x_prompt, mem_prompt, wts), _encode(x_sample, mem_sample, wts))
```

```python
import functools
import math

import numpy as np
import jax
import jax.numpy as jnp
from jax import lax
from jax.experimental import pallas as pl
from jax.experimental.pallas import tpu as pltpu

D_MODEL = 1024
H_GDN = 8
DK_GDN = 128
DV_GDN = 128
W_GDN = H_GDN * DV_GDN
CONV_K = 5
CHUNK = 64
H_MLA = 8
Q_LORA = 384
KV_LORA = 256
D_NOPE = 128
D_ROPE = 64
D_V_MLA = 128
W_MLA = H_MLA * D_V_MLA
ROPE_THETA = 10000.0
H_MEM = 4
D_MEM = 128
W_MEM = H_MEM * D_MEM
N_BRANCH = 3
EPS = 1e-6
W_QKV = H_GDN * (2 * DK_GDN + DV_GDN)
SPLIT_SIZES = (W_QKV, 4 * H_GDN, W_GDN, Q_LORA, KV_LORA + D_ROPE, W_MLA, W_MEM, W_MEM, N_BRANCH * D_MODEL)

LANES = 128
SUBLANES = 8
D_QK_PAD = 256
VMEM_LIMIT = 56 * 1024 * 1024

F32 = jnp.float32
BF16 = jnp.bfloat16

MK_INCL, MK_STRICT, MK_PAIR, MK_MERGE0 = 0, 1, 2, 3
MERGE_SIZES = (2, 4, 8, 16, 32)
MK_EYE = MK_MERGE0 + len(MERGE_SIZES)


def _rms(x, gain):
    return x * lax.rsqrt(jnp.mean(x * x, axis=-1, keepdims=True) + EPS) * gain


def _sigmoid(x):
    return 1.0 / (1.0 + jnp.exp(-x))


def _silu(x):
    return x * _sigmoid(x)


def _dot(a, b):
    return jnp.dot(a, b, preferred_element_type=F32)


def _dot_nt(a, b):
    return lax.dot_general(a, b, (((1,), (1,)), ((), ())), preferred_element_type=F32)


def _dot_tn(a, b):
    return lax.dot_general(a, b, (((0,), (0,)), ((), ())), preferred_element_type=F32)


def _proj_kernel(x_ref, gain_ref, w_ref, o_ref):
    h = _rms(x_ref[...], gain_ref[...]).astype(BF16)
    o_ref[...] = _dot(h, w_ref[...]).astype(o_ref.dtype)


def _proj(x2, gain, w2, tm):
    t = x2.shape[0]
    ng, _, n = w2.shape
    return pl.pallas_call(
        _proj_kernel,
        grid=(ng, t // tm),
        in_specs=[
            pl.BlockSpec((tm, D_MODEL), lambda j, i: (i, 0)),
            pl.BlockSpec((1, D_MODEL), lambda j, i: (0, 0)),
            pl.BlockSpec((None, D_MODEL, n), lambda j, i: (j, 0, 0)),
        ],
        out_specs=pl.BlockSpec((None, tm, n), lambda j, i: (j, i, 0)),
        out_shape=jax.ShapeDtypeStruct((ng, t, n), BF16),
        compiler_params=pltpu.CompilerParams(
            dimension_semantics=("arbitrary", "arbitrary"), vmem_limit_bytes=VMEM_LIMIT),
        name="proj",
    )(x2, gain, w2)


W_SMALL = Q_LORA + KV_LORA + LANES + LANES
MLA_Q_SCALE = (D_NOPE + D_ROPE) ** -0.5 * math.log2(math.e)


def _mla_prep_kernel(x_ref, gain_ref, w_ref, qg_ref, kvg_ref, wq_ref, wkv_ref, tabq_ref, tabk_ref,
                     alog_ref, dtb_ref, qm_ref, kf_ref, v_ref, gb_ref):
    h = _rms(x_ref[...], gain_ref[...]).astype(BF16)
    p = _dot(h, w_ref[...])
    cq = p[:, :Q_LORA]
    kvl = p[:, Q_LORA:Q_LORA + KV_LORA]
    kpe = p[:, Q_LORA + KV_LORA:Q_LORA + KV_LORA + LANES]
    ab = p[:, Q_LORA + KV_LORA + LANES:]
    scale = MLA_Q_SCALE

    qm = _dot(_rms(cq, qg_ref[...]).astype(BF16), wq_ref[...])
    tabq = tabq_ref[...]
    for hh in range(H_MLA):
        c0 = hh * D_QK_PAD
        qm_ref[:, c0:c0 + D_NOPE] = (qm[:, c0:c0 + D_NOPE] * scale).astype(BF16)
        t = qm[:, c0 + D_NOPE:c0 + D_QK_PAD] * tabq
        qm_ref[:, c0 + D_NOPE:c0 + D_QK_PAD] = (t + pltpu.roll(t, D_ROPE, 1)).astype(BF16)

    kv = _dot(_rms(kvl, kvg_ref[...]).astype(BF16), wkv_ref[...])
    t = kpe * tabk_ref[...]
    t = t + pltpu.roll(t, D_ROPE, 1)
    lane = lax.broadcasted_iota(jnp.int32, t.shape, 1)
    kpe_rot = jnp.where(lane < D_ROPE, t, 0.0).astype(BF16)
    for hh in range(H_MLA):
        c0 = hh * D_QK_PAD
        kf_ref[:, c0:c0 + D_NOPE] = kv[:, hh * D_NOPE:(hh + 1) * D_NOPE].astype(BF16)
        kf_ref[:, c0 + D_NOPE:c0 + D_QK_PAD] = kpe_rot
    v_ref[...] = kv[:, W_MLA:].astype(BF16)

    a = ab + dtb_ref[...]
    softplus = jnp.maximum(a, 0.0) + jnp.log(1.0 + jnp.exp(-jnp.abs(a)))
    g = -jnp.exp(alog_ref[...]) * softplus
    beta = _sigmoid(ab)
    tm = g.shape[0]
    rin = lax.broadcasted_iota(jnp.int32, g.shape, 0) % CHUNK
    pre, suf = g, g
    sh = 1
    while sh < CHUNK:
        pre = pre + jnp.where(rin >= sh, pltpu.roll(pre, sh, 0), 0.0)
        suf = suf + jnp.where(rin < CHUNK - sh, pltpu.roll(suf, tm - sh, 0), 0.0)
        sh *= 2
    gb_ref[...] = jnp.where(lane < H_GDN, pre, jnp.where(lane < 2 * H_GDN, suf,
                                                          jnp.where(lane < 4 * H_GDN, beta, 0.0)))


def _mla_prep(x2, gain, w_small, qg, kvg, wq, wkv, tabq, tabk, alog, dtb, tm, s):
    t = x2.shape[0]
    npos = s // tm
    const = lambda i: (0, 0)
    row = lambda i: (i, 0)
    return pl.pallas_call(
        _mla_prep_kernel,
        grid=(t // tm,),
        in_specs=[
            pl.BlockSpec((tm, D_MODEL), row),
            pl.BlockSpec((1, D_MODEL), const),
            pl.BlockSpec((D_MODEL, W_SMALL), const),
            pl.BlockSpec((1, Q_LORA), const),
            pl.BlockSpec((1, KV_LORA), const),
            pl.BlockSpec((Q_LORA, H_MLA * D_QK_PAD), const),
            pl.BlockSpec((KV_LORA, 2 * W_MLA), const),
            pl.BlockSpec((tm, LANES), lambda i: (i % npos, 0)),
            pl.BlockSpec((tm, LANES), lambda i: (i % npos, 0)),
            pl.BlockSpec((1, LANES), const),
            pl.BlockSpec((1, LANES), const),
        ],
        out_specs=[
            pl.BlockSpec((tm, H_MLA * D_QK_PAD), row),
            pl.BlockSpec((tm, H_MLA * D_QK_PAD), row),
            pl.BlockSpec((tm, W_MLA), row),
            pl.BlockSpec((tm, LANES), row),
        ],
        out_shape=[
            jax.ShapeDtypeStruct((t, H_MLA * D_QK_PAD), BF16),
            jax.ShapeDtypeStruct((t, H_MLA * D_QK_PAD), BF16),
            jax.ShapeDtypeStruct((t, W_MLA), BF16),
            jax.ShapeDtypeStruct((t, LANES), F32),
        ],
        compiler_params=pltpu.CompilerParams(
            dimension_semantics=("arbitrary",), vmem_limit_bytes=VMEM_LIMIT),
        name="mla_prep",
    )(x2, gain, w_small, qg, kvg, wq, wkv, tabq, tabk, alog, dtb)


MLA_SUB = 256


def _mla_attn_kernel(q_ref, k_ref, v_ref, z_ref, o_ref):
    tq = q_ref.shape[0]
    sub = math.gcd(tq, MLA_SUB)
    n = tq // sub
    k = k_ref[...]
    v = v_ref[...]
    v_ext = jnp.concatenate([v, jnp.ones_like(v)], axis=1)

    def scores(i):
        return _dot_nt(q_ref[i * sub:(i + 1) * sub, :], k)

    def finish(i, s):
        p = jnp.exp2(s - jnp.max(s, axis=-1, keepdims=True)).astype(BF16)
        ol = _dot(p, v_ext)
        o = ol[:, :D_V_MLA] / ol[:, D_V_MLA:]
        z = z_ref[i * sub:(i + 1) * sub, :].astype(F32)
        o_ref[i * sub:(i + 1) * sub, :] = (o * _silu(z)).astype(o_ref.dtype)

    s_next = scores(0)
    for i in range(n):
        s_cur = s_next
        if i + 1 < n:
            s_next = scores(i + 1)
        finish(i, s_cur)


def _mla_attn(qm, kf, v, zq, b, s, tq):
    t = b * s
    nq = s // tq
    return pl.pallas_call(
        _mla_attn_kernel,
        grid=(b, H_MLA, nq),
        in_specs=[
            pl.BlockSpec((tq, D_QK_PAD), lambda bi, h, i: (bi * nq + i, h)),
            pl.BlockSpec((s, D_QK_PAD), lambda bi, h, i: (bi, h)),
            pl.BlockSpec((s, D_V_MLA), lambda bi, h, i: (bi, h)),
            pl.BlockSpec((None, tq, LANES), lambda bi, h, i: (1, bi * nq + i, W_GDN // LANES + h)),
        ],
        out_specs=pl.BlockSpec((tq, D_V_MLA), lambda bi, h, i: (bi * nq + i, h)),
        out_shape=jax.ShapeDtypeStruct((t, W_MLA), BF16),
        compiler_params=pltpu.CompilerParams(
            dimension_semantics=("arbitrary", "arbitrary", "arbitrary"), vmem_limit_bytes=VMEM_LIMIT),
        name="mla_attn",
    )(qm, kf, v, zq)


def _gdn_masks():
    i = np.arange(CHUNK)[:, None]
    j = np.arange(CHUNK)[None, :]
    per_dir = []
    for d in range(2):
        before = (i > j) if d == 0 else (i < j)
        out = [(i == j) | before,
               before,
               before & (i // 2 == j // 2)]
        for sz in MERGE_SIZES:
            out.append(before & (i // (2 * sz) == j // (2 * sz)) & (i // sz != j // sz))
        out.append(i == j)
        per_dir.append(np.stack(out))
    return np.concatenate(per_dir, axis=2).astype(np.float32)


GDN_HEADS_PER_STEP = 2
GDN_CHUNKS_PER_ITER = 8


def _gdn_kernel(q_ref, k_ref, v_ref, z_ref, gb_ref, cwq_ref, cwk_ref, cwv_ref, gain_ref, mk_ref, o_ref,
                xpad, qs, ks, vs, gc_s, bt_s, mp_s, n_s, r_s, eg_s, o_s):
    s = q_ref.shape[0]
    nc = s // CHUNK
    pad = SUBLANES
    halo = (CONV_K - 1) // 2
    group = math.gcd(nc, GDN_CHUNKS_PER_ITER)
    nheads = GDN_HEADS_PER_STEP

    def rows(c):
        return pl.ds(pl.multiple_of(c * CHUNK, CHUNK), CHUNK)

    def group_loop(body):
        def grouped(i, carry):
            body([i * group + t for t in range(group)])
            return carry
        lax.fori_loop(0, nc // group, grouped, 0)

    def chunk_loop(body):
        def each(cs):
            for c in cs:
                body(c)
        group_loop(each)

    zeros = jnp.zeros((pad, LANES), F32)
    for a in range(3):
        xpad[a, 0:pad, :] = zeros
        xpad[a, pad + s:2 * pad + s, :] = zeros
    lane = lax.broadcasted_iota(jnp.int32, (CHUNK, LANES), 1)
    fwd_lanes = lane < CHUNK

    def block_diag(z):
        return jnp.concatenate([jnp.where(fwd_lanes, z, 0.0), jnp.where(fwd_lanes, 0.0, z)], axis=0).astype(BF16)

    for hh in range(nheads):
        head = pl.program_id(1) * nheads + hh
        cols = slice(hh * LANES, (hh + 1) * LANES)

        convs = ((q_ref, cwq_ref, qs, "q"), (k_ref, cwk_ref, ks, "k"), (v_ref, cwv_ref, vs, "v"))
        for a, (src, _, _, _) in enumerate(convs):
            xpad[a, pad:pad + s, :] = src[:, cols].astype(F32)
        cws = [cw_ref[:, cols] for _, cw_ref, _, _ in convs]

        def conv_body(c, head=head, cws=cws):
            base = pl.multiple_of(c * CHUNK, CHUNK)
            for a, (_, _, dst, kind) in enumerate(convs):
                cw = cws[a]
                y = cw[0:1, :] * xpad[a, pl.ds(base + pad - halo, CHUNK), :]
                for j in range(1, CONV_K):
                    y = y + cw[j:j + 1, :] * xpad[a, pl.ds(base + pad - halo + j, CHUNK), :]
                y = _silu(y)
                if kind != "v":
                    y = y * lax.rsqrt(jnp.sum(y * y, axis=-1, keepdims=True) + EPS)
                if kind == "q":
                    y = y * DK_GDN ** -0.5
                dst[pl.ds(base, CHUNK), :] = y
            gbc = gb_ref[pl.ds(base, CHUNK), :]
            for d in range(2):
                for ref, cidx in ((gc_s, d * H_GDN + head), (bt_s, (2 + d) * H_GDN + head)):
                    col = jnp.sum(jnp.where(lane == cidx, gbc, 0.0), axis=-1, keepdims=True)
                    ref[d, pl.ds(base, CHUNK), :] = jnp.broadcast_to(col, gbc.shape)

        chunk_loop(conv_body)

        def prep_group(cs, hh=hh):
            n = len(cs)
            rs = [rows(c) for c in cs]
            q = [qs[r, :] for r in rs]
            k = [ks[r, :] for r in rs]
            gc = [[gc_s[d, r, :] for d in range(2)] for r in rs]
            bt = [[bt_s[d, r, :] for d in range(2)] for r in rs]
            k16 = [t.astype(BF16) for t in k]
            qkk = [_dot_nt(jnp.concatenate([q[i].astype(BF16), k16[i]], axis=0),
                           jnp.concatenate([k16[i], k16[i]], axis=0)) for i in range(n)]
            e2, l2, x2 = [], [], []
            for i in range(n):
                gcol = jnp.where(fwd_lanes, gc[i][0], gc[i][1])
                grow = jnp.transpose(jnp.concatenate(gc[i], axis=0))[:CHUNK, :]
                e = jnp.exp(jnp.where(mk_ref[MK_INCL] > 0.5, gcol - grow, -jnp.inf))
                lm = qkk[i][CHUNK:] * jnp.where(fwd_lanes, bt[i][0], bt[i][1]) * e * mk_ref[MK_STRICT]
                e2.append(e)
                l2.append(lm)
                x2.append(mk_ref[MK_EYE] - lm * mk_ref[MK_PAIR])
            for t in range(len(MERGE_SIZES)):
                y2 = [_dot((l2[i] * mk_ref[MK_MERGE0 + t]).astype(BF16), block_diag(x2[i])) for i in range(n)]
                x2 = [x2[i] - _dot(x2[i].astype(BF16), block_diag(y2[i])) for i in range(n)]
            eg = [[jnp.exp(g) for g in gc[i]] for i in range(n)]
            gl = [[gc[i][0][CHUNK - 1:CHUNK, :], gc[i][1][0:1, :]] for i in range(n)]
            wu = []
            for i in range(n):
                v = vs[rs[i], :]
                rhs = jnp.concatenate(
                    [jnp.concatenate([(k[i] * bt[i][d] * eg[i][d]).astype(BF16), (v * bt[i][d]).astype(BF16)], axis=1)
                     for d in range(2)], axis=0)
                wu.append(_dot(block_diag(x2[i]), rhs).astype(BF16))
            iwu = [_dot(block_diag(qkk[i][:CHUNK] * e2[i]), wu[i]) for i in range(n)]
            kwu = [[_dot_tn((k[i] * jnp.exp(gl[i][d] - gc[i][d])).astype(BF16), wu[i][d * CHUNK:(d + 1) * CHUNK])
                    for d in range(2)] for i in range(n)]
            for i, c in enumerate(cs):
                for d in range(2):
                    iw = iwu[i][d * CHUNK:(d + 1) * CHUNK]
                    mp_s[hh, d, c, 0:DK_GDN, :] = (-kwu[i][d][:, :DK_GDN]).astype(BF16)
                    mp_s[hh, d, c, DK_GDN:DK_GDN + CHUNK, :] = (q[i] * eg[i][d] - iw[:, :DK_GDN]).astype(BF16)
                    n_s[hh, d, c] = kwu[i][d][:, DK_GDN:]
                    r_s[hh, d, c] = iw[:, DK_GDN:]
                    eg_s[hh, d, c] = jnp.broadcast_to(jnp.exp(gl[i][d]), (SUBLANES, LANES))

        group_loop(prep_group)

    def scan_body(i, states):
        new_states = []
        for hh in range(nheads):
            for d in range(2):
                c = i if d == 0 else nc - 1 - i
                st = states[2 * hh + d]
                x = _dot(mp_s[hh, d, c], st.astype(BF16))
                o_s[hh, d, rows(c), :] = x[DK_GDN:] + r_s[hh, d, c]
                new_states.append(st * eg_s[hh, d, c, 0:1, :] + x[:DK_GDN] + n_s[hh, d, c])
        return tuple(new_states)

    st0 = jnp.zeros((DK_GDN, DV_GDN), F32)
    lax.fori_loop(0, nc, scan_body, (st0,) * (2 * nheads))

    gain = gain_ref[...]
    for hh in range(nheads):
        cols = slice(hh * LANES, (hh + 1) * LANES)

        def out_body(c, hh=hh, cols=cols):
            r = rows(c)
            o = o_s[hh, 0, r, :] + o_s[hh, 1, r, :]
            o_ref[r, cols] = (_rms(o, gain) * _silu(z_ref[r, cols].astype(F32))).astype(o_ref.dtype)

        chunk_loop(out_body)


def _gdn(pz, gb, conv_w, gdn_gain, masks, b, s):
    t = b * s
    nc = s // CHUNK
    nh = GDN_HEADS_PER_STEP
    w = nh * LANES
    ng = H_GDN // nh
    col = lambda off: (lambda bi, h: (0, bi, off + h))
    cwcol = lambda off: (lambda bi, h: (0, off + h))
    return pl.pallas_call(
        _gdn_kernel,
        grid=(b, ng),
        in_specs=[
            pl.BlockSpec((None, s, w), col(0)),
            pl.BlockSpec((None, s, w), col(ng)),
            pl.BlockSpec((None, s, w), col(2 * ng)),
            pl.BlockSpec((None, s, w), lambda bi, h: (1, bi, h)),
            pl.BlockSpec((s, LANES), lambda bi, h: (bi, 0)),
            pl.BlockSpec((CONV_K, w), cwcol(0)),
            pl.BlockSpec((CONV_K, w), cwcol(ng)),
            pl.BlockSpec((CONV_K, w), cwcol(2 * ng)),
            pl.BlockSpec((1, DV_GDN), lambda bi, h: (0, 0)),
            pl.BlockSpec(masks.shape, lambda bi, h: (0, 0, 0)),
        ],
        out_specs=pl.BlockSpec((s, w), lambda bi, h: (bi, h)),
        out_shape=jax.ShapeDtypeStruct((t, W_GDN), BF16),
        scratch_shapes=[
            pltpu.VMEM((3, s + 2 * SUBLANES, LANES), F32),
            pltpu.VMEM((s, LANES), F32),
            pltpu.VMEM((s, LANES), F32),
            pltpu.VMEM((s, LANES), F32),
            pltpu.VMEM((2, s, LANES), F32),
            pltpu.VMEM((2, s, LANES), F32),
            pltpu.VMEM((nh, 2, nc, DK_GDN + CHUNK, DV_GDN), BF16),
            pltpu.VMEM((nh, 2, nc, DK_GDN, DV_GDN), F32),
            pltpu.VMEM((nh, 2, nc, CHUNK, DV_GDN), F32),
            pltpu.VMEM((nh, 2, nc, SUBLANES, LANES), F32),
            pltpu.VMEM((nh, 2, s, LANES), F32),
        ],
        compiler_params=pltpu.CompilerParams(
            dimension_semantics=("arbitrary", "arbitrary"), vmem_limit_bytes=VMEM_LIMIT),
        name="gdn",
    )(pz, pz, pz, pz, gb, conv_w, conv_w, conv_w, gdn_gain, masks)


def _mem_kv_kernel(m_ref, gain_ref, w_ref, o_ref):
    o_ref[...] = _dot(_rms(m_ref[...], gain_ref[...]).astype(BF16), w_ref[...]).astype(o_ref.dtype)


def _mem_kv(mem, gain, w):
    b, n, _ = mem.shape
    return pl.pallas_call(
        _mem_kv_kernel,
        grid=(b,),
        in_specs=[
            pl.BlockSpec((None, n, D_MODEL), lambda bi: (bi, 0, 0)),
            pl.BlockSpec((1, D_MODEL), lambda bi: (0, 0)),
            pl.BlockSpec((D_MODEL, 2 * W_MEM), lambda bi: (0, 0)),
        ],
        out_specs=pl.BlockSpec((None, n, 2 * W_MEM), lambda bi: (bi, 0, 0)),
        out_shape=jax.ShapeDtypeStruct((b, n, 2 * W_MEM), BF16),
        compiler_params=pltpu.CompilerParams(
            dimension_semantics=("arbitrary",), vmem_limit_bytes=VMEM_LIMIT),
        name="mem_kv",
    )(mem, gain, w)


def _out_kernel(x_ref, again_ref, wg_ref, og_ref, om_ref, zq_ref, kvm_ref, wbg_ref, wbm_ref, wbe_ref, wo_ref,
                fgain_ref, o_ref):
    x = x_ref[...]
    h = _rms(x, again_ref[...]).astype(BF16)
    gates = _sigmoid(_dot(h, wg_ref[...]))

    zq = zq_ref[...]
    kvm = kvm_ref[...]
    heads = []
    for hh in range(H_MEM):
        c0 = hh * D_MEM
        sc = _dot_nt(zq[:, W_MEM + c0:W_MEM + c0 + D_MEM], kvm[:, c0:c0 + D_MEM]) * D_MEM ** -0.5
        p = jnp.exp(sc - jnp.max(sc, axis=-1, keepdims=True))
        l = jnp.sum(p, axis=-1, keepdims=True)
        o = _dot(p.astype(BF16), kvm[:, W_MEM + c0:W_MEM + c0 + D_MEM]) / l
        heads.append((o * _silu(zq[:, c0:c0 + D_MEM].astype(F32))).astype(BF16))
    o_mem = jnp.concatenate(heads, axis=1)

    merged = (gates[:, :D_MODEL] * _dot(og_ref[...], wbg_ref[...])
              + gates[:, D_MODEL:2 * D_MODEL] * _dot(om_ref[...], wbm_ref[...])
              + gates[:, 2 * D_MODEL:] * _dot(o_mem, wbe_ref[...]))
    y = x + _dot(merged.astype(BF16), wo_ref[...])
    o_ref[...] = _rms(y, fgain_ref[...])


def _out(x2, again, wg, og, om, pz, kvm, wbg, wbm, wbe, wo, fgain, s, tm):
    t = x2.shape[0]
    nt = s // tm
    const = lambda i: (0, 0)
    row = lambda i: (i, 0)
    return pl.pallas_call(
        _out_kernel,
        grid=(t // tm,),
        in_specs=[
            pl.BlockSpec((tm, D_MODEL), row),
            pl.BlockSpec((1, D_MODEL), const),
            pl.BlockSpec((D_MODEL, N_BRANCH * D_MODEL), const),
            pl.BlockSpec((tm, W_GDN), row),
            pl.BlockSpec((tm, W_MLA), row),
            pl.BlockSpec((None, tm, 2 * W_MEM), lambda i: (1, i, (W_GDN + W_MLA) // (2 * W_MEM))),
            pl.BlockSpec((None, kvm.shape[1], 2 * W_MEM), lambda i: (i // nt, 0, 0)),
            pl.BlockSpec((W_GDN, D_MODEL), const),
            pl.BlockSpec((W_MLA, D_MODEL), const),
            pl.BlockSpec((W_MEM, D_MODEL), const),
            pl.BlockSpec((D_MODEL, D_MODEL), const),
            pl.BlockSpec((1, D_MODEL), const),
        ],
        out_specs=pl.BlockSpec((tm, D_MODEL), row),
        out_shape=jax.ShapeDtypeStruct((t, D_MODEL), F32),
        compiler_params=pltpu.CompilerParams(
            dimension_semantics=("arbitrary",), vmem_limit_bytes=VMEM_LIMIT),
        name="out",
    )(x2, again, wg, og, om, pz, kvm, wbg, wbm, wbe, wo, fgain)


def _prep_weights(attn_norm_gain, w_in, conv_w, a_log, dt_bias, gdn_norm_gain, q_norm_gain, w_q_up,
                  kv_norm_gain, w_kv_up, mem_norm_gain, w_mem_kv, w_br_gdn, w_br_mla, w_br_mem, w_out,
                  final_norm_gain):
    w = w_in[0]
    offs = np.concatenate([[0], np.cumsum(SPLIT_SIZES)])
    seg = lambda n: w[:, int(offs[n]):int(offs[n + 1])]
    w_qkv, w_ab, w_zg, w_cq, w_ckv, w_zm, w_qmem, w_zmem, w_gate = (seg(n) for n in range(9))
    half = D_ROPE // 2

    def rope_cols(wpe):
        x1, x2 = wpe[..., :half], wpe[..., half:]
        return jnp.concatenate([x1, x2, x2, x1], axis=-1)

    w_big = jnp.stack([w_qkv, jnp.concatenate([w_zg, w_zm, w_zmem, w_qmem], axis=1)]).astype(BF16)
    w_small = jnp.concatenate([
        w_cq, w_ckv[:, :KV_LORA], rope_cols(w_ckv[:, KV_LORA:]),
        w_ab, jnp.zeros((D_MODEL, LANES - 4 * H_GDN), w.dtype)], axis=1).astype(BF16)

    wq = w_q_up[0].reshape(Q_LORA, H_MLA, D_NOPE + D_ROPE)
    wq = jnp.concatenate([wq[..., :D_NOPE], rope_cols(wq[..., D_NOPE:])], axis=-1)
    wq = wq.reshape(Q_LORA, H_MLA * D_QK_PAD).astype(BF16)
    wkv = w_kv_up[0].reshape(KV_LORA, H_MLA, D_NOPE + D_V_MLA)
    wkv = jnp.concatenate([wkv[..., :D_NOPE].reshape(KV_LORA, W_MLA),
                           wkv[..., D_NOPE:].reshape(KV_LORA, W_MLA)], axis=1).astype(BF16)

    pad_lanes = lambda v: jnp.concatenate([v.reshape(1, -1), jnp.zeros((1, LANES - v.size), F32)], axis=1)
    return dict(
        again=attn_norm_gain[0].reshape(1, D_MODEL), w_big=w_big, w_small=w_small, w_gate=w_gate.astype(BF16),
        qg=q_norm_gain[0].reshape(1, Q_LORA), kvg=kv_norm_gain[0].reshape(1, KV_LORA), wq=wq, wkv=wkv,
        alog=pad_lanes(a_log[0]), dtb=pad_lanes(dt_bias[0]),
        conv_w=conv_w[0], gdn_gain=gdn_norm_gain[0].reshape(1, DV_GDN),
        mgain=mem_norm_gain[0].reshape(1, D_MODEL), w_mem_kv=w_mem_kv[0].astype(BF16),
        wbg=w_br_gdn[0].astype(BF16), wbm=w_br_mla[0].astype(BF16), wbe=w_br_mem[0].astype(BF16),
        wo=w_out[0].astype(BF16), fgain=final_norm_gain.reshape(1, D_MODEL),
        masks=jnp.asarray(_gdn_masks()),
    )


def _rope_tables(s):
    half = D_ROPE // 2
    inv_freq = ROPE_THETA ** (-jnp.arange(half, dtype=F32) / half)
    ang = jnp.arange(s, dtype=jnp.int32).astype(F32)[:, None] * inv_freq[None, :]
    cos, sin = jnp.cos(ang), jnp.sin(ang)
    tabk = jnp.concatenate([cos, cos, -sin, sin], axis=1)
    return tabk * MLA_Q_SCALE, tabk


def _tile(s, want):
    return math.gcd(s, want)


def _encode(x, mem, wts):
    b, s, _ = x.shape
    x2 = x.reshape(b * s, D_MODEL)
    tabq, tabk = _rope_tables(s)
    pz = _proj(x2, wts["again"], wts["w_big"], _tile(s, 512))
    qm, kf, v, gb = _mla_prep(x2, wts["again"], wts["w_small"], wts["qg"], wts["kvg"], wts["wq"], wts["wkv"],
                              tabq, tabk, wts["alog"], wts["dtb"], _tile(s, 256), s)
    o_mla = _mla_attn(qm, kf, v, pz, b, s, _tile(s, 2048))
    o_gdn = _gdn(pz, gb, wts["conv_w"], wts["gdn_gain"], wts["masks"], b, s)
    kvm = _mem_kv(mem, wts["mgain"], wts["w_mem_kv"])
    y = _out(x2, wts["again"], wts["w_gate"], o_gdn, o_mla, pz, kvm, wts["wbg"], wts["wbm"], wts["wbe"],
             wts["wo"], wts["fgain"], s, _tile(s, 256))
    return y.reshape(b, s, D_MODEL)


def kernel(x_prompt, x_sample, mem_prompt, mem_sample, attn_norm_gain, w_in, conv_w, a_log, dt_bias, gdn_norm_gain, q_norm_gain, w_q_up, kv_norm_gain, w_kv_up, mem_norm_gain, w_mem_kv, w_br_gdn, w_br_mla, w_br_mem, w_out, final_norm_gain):
    wts = _prep_weights(attn_norm_gain, w_in, conv_w, a_log, dt_bias, gdn_norm_gain, q_norm_gain, w_q_up,
                        kv_norm_gain, w_kv_up, mem_norm_gain, w_mem_kv, w_br_gdn, w_br_mla, w_br_mem, w_out,
                        final_norm_gain)
    return (_encode(x_prompt, mem_prompt, wts), _encode(x_sample, mem_sample, wts))
```

```python
import functools
import math

import numpy as np
import jax
import jax.numpy as jnp
from jax import lax
from jax.experimental import pallas as pl
from jax.experimental.pallas import tpu as pltpu

D_MODEL = 1024
H_GDN = 8
DK_GDN = 128
DV_GDN = 128
W_GDN = H_GDN * DV_GDN
CONV_K = 5
CHUNK = 64
H_MLA = 8
Q_LORA = 384
KV_LORA = 256
D_NOPE = 128
D_ROPE = 64
D_V_MLA = 128
W_MLA = H_MLA * D_V_MLA
ROPE_THETA = 10000.0
H_MEM = 4
D_MEM = 128
W_MEM = H_MEM * D_MEM
N_BRANCH = 3
EPS = 1e-6
W_QKV = H_GDN * (2 * DK_GDN + DV_GDN)
SPLIT_SIZES = (W_QKV, 4 * H_GDN, W_GDN, Q_LORA, KV_LORA + D_ROPE, W_MLA, W_MEM, W_MEM, N_BRANCH * D_MODEL)

LANES = 128
SUBLANES = 8
D_QK_PAD = 256
VMEM_LIMIT = 56 * 1024 * 1024

F32 = jnp.float32
BF16 = jnp.bfloat16

MK_INCL, MK_STRICT, MK_PAIR, MK_MERGE0 = 0, 1, 2, 3
MERGE_SIZES = (2, 4, 8, 16, 32)
MK_EYE = MK_MERGE0 + len(MERGE_SIZES)


def _rms(x, gain):
    return x * lax.rsqrt(jnp.mean(x * x, axis=-1, keepdims=True) + EPS) * gain


def _sigmoid(x):
    return 1.0 / (1.0 + jnp.exp(-x))


def _silu(x):
    return x * _sigmoid(x)


def _dot(a, b):
    return jnp.dot(a, b, preferred_element_type=F32)


def _dot_nt(a, b):
    return lax.dot_general(a, b, (((1,), (1,)), ((), ())), preferred_element_type=F32)


def _dot_tn(a, b):
    return lax.dot_general(a, b, (((0,), (0,)), ((), ())), preferred_element_type=F32)


def _proj_kernel(x_ref, gain_ref, w_ref, o_ref):
    h = _rms(x_ref[...], gain_ref[...]).astype(BF16)
    o_ref[...] = _dot(h, w_ref[...]).astype(o_ref.dtype)


def _proj(x2, gain, w2, tm):
    t = x2.shape[0]
    ng, _, n = w2.shape
    return pl.pallas_call(
        _proj_kernel,
        grid=(ng, t // tm),
        in_specs=[
            pl.BlockSpec((tm, D_MODEL), lambda j, i: (i, 0)),
            pl.BlockSpec((1, D_MODEL), lambda j, i: (0, 0)),
            pl.BlockSpec((None, D_MODEL, n), lambda j, i: (j, 0, 0)),
        ],
        out_specs=pl.BlockSpec((None, tm, n), lambda j, i: (j, i, 0)),
        out_shape=jax.ShapeDtypeStruct((ng, t, n), BF16),
        compiler_params=pltpu.CompilerParams(
            dimension_semantics=("arbitrary", "arbitrary"), vmem_limit_bytes=VMEM_LIMIT),
        name="proj",
    )(x2, gain, w2)


W_SMALL = Q_LORA + KV_LORA + LANES + LANES
MLA_Q_SCALE = (D_NOPE + D_ROPE) ** -0.5 * math.log2(math.e)


def _mla_prep_kernel(x_ref, gain_ref, w_ref, qg_ref, kvg_ref, wq_ref, wkv_ref, tabq_ref, tabk_ref,
                     alog_ref, dtb_ref, qm_ref, kf_ref, v_ref, gb_ref):
    h = _rms(x_ref[...], gain_ref[...]).astype(BF16)
    p = _dot(h, w_ref[...])
    cq = p[:, :Q_LORA]
    kvl = p[:, Q_LORA:Q_LORA + KV_LORA]
    kpe = p[:, Q_LORA + KV_LORA:Q_LORA + KV_LORA + LANES]
    ab = p[:, Q_LORA + KV_LORA + LANES:]
    scale = MLA_Q_SCALE

    qm = _dot(_rms(cq, qg_ref[...]).astype(BF16), wq_ref[...])
    tabq = tabq_ref[...]
    for hh in range(H_MLA):
        c0 = hh * D_QK_PAD
        qm_ref[:, c0:c0 + D_NOPE] = (qm[:, c0:c0 + D_NOPE] * scale).astype(BF16)
        t = qm[:, c0 + D_NOPE:c0 + D_QK_PAD] * tabq
        qm_ref[:, c0 + D_NOPE:c0 + D_QK_PAD] = (t + pltpu.roll(t, D_ROPE, 1)).astype(BF16)

    kv = _dot(_rms(kvl, kvg_ref[...]).astype(BF16), wkv_ref[...])
    t = kpe * tabk_ref[...]
    t = t + pltpu.roll(t, D_ROPE, 1)
    lane = lax.broadcasted_iota(jnp.int32, t.shape, 1)
    kpe_rot = jnp.where(lane < D_ROPE, t, 0.0).astype(BF16)
    for hh in range(H_MLA):
        c0 = hh * D_QK_PAD
        kf_ref[:, c0:c0 + D_NOPE] = kv[:, hh * D_NOPE:(hh + 1) * D_NOPE].astype(BF16)
        kf_ref[:, c0 + D_NOPE:c0 + D_QK_PAD] = kpe_rot
    v_ref[...] = kv[:, W_MLA:].astype(BF16)

    a = ab + dtb_ref[...]
    softplus = jnp.maximum(a, 0.0) + jnp.log(1.0 + jnp.exp(-jnp.abs(a)))
    g = -jnp.exp(alog_ref[...]) * softplus
    beta = _sigmoid(ab)
    tm = g.shape[0]
    rin = lax.broadcasted_iota(jnp.int32, g.shape, 0) % CHUNK
    pre, suf = g, g
    sh = 1
    while sh < CHUNK:
        pre = pre + jnp.where(rin >= sh, pltpu.roll(pre, sh, 0), 0.0)
        suf = suf + jnp.where(rin < CHUNK - sh, pltpu.roll(suf, tm - sh, 0), 0.0)
        sh *= 2
    gb_ref[...] = jnp.where(lane < H_GDN, pre, jnp.where(lane < 2 * H_GDN, suf,
                                                          jnp.where(lane < 4 * H_GDN, beta, 0.0)))


def _mla_prep(x2, gain, w_small, qg, kvg, wq, wkv, tabq, tabk, alog, dtb, tm, s):
    t = x2.shape[0]
    npos = s // tm
    const = lambda i: (0, 0)
    row = lambda i: (i, 0)
    return pl.pallas_call(
        _mla_prep_kernel,
        grid=(t // tm,),
        in_specs=[
            pl.BlockSpec((tm, D_MODEL), row),
            pl.BlockSpec((1, D_MODEL), const),
            pl.BlockSpec((D_MODEL, W_SMALL), const),
            pl.BlockSpec((1, Q_LORA), const),
            pl.BlockSpec((1, KV_LORA), const),
            pl.BlockSpec((Q_LORA, H_MLA * D_QK_PAD), const),
            pl.BlockSpec((KV_LORA, 2 * W_MLA), const),
            pl.BlockSpec((tm, LANES), lambda i: (i % npos, 0)),
            pl.BlockSpec((tm, LANES), lambda i: (i % npos, 0)),
            pl.BlockSpec((1, LANES), const),
            pl.BlockSpec((1, LANES), const),
        ],
        out_specs=[
            pl.BlockSpec((tm, H_MLA * D_QK_PAD), row),
            pl.BlockSpec((tm, H_MLA * D_QK_PAD), row),
            pl.BlockSpec((tm, W_MLA), row),
            pl.BlockSpec((tm, LANES), row),
        ],
        out_shape=[
            jax.ShapeDtypeStruct((t, H_MLA * D_QK_PAD), BF16),
            jax.ShapeDtypeStruct((t, H_MLA * D_QK_PAD), BF16),
            jax.ShapeDtypeStruct((t, W_MLA), BF16),
            jax.ShapeDtypeStruct((t, LANES), F32),
        ],
        compiler_params=pltpu.CompilerParams(
            dimension_semantics=("arbitrary",), vmem_limit_bytes=VMEM_LIMIT),
        name="mla_prep",
    )(x2, gain, w_small, qg, kvg, wq, wkv, tabq, tabk, alog, dtb)


MLA_SUB = 256


def _mla_attn_kernel(q_ref, k_ref, v_ref, z_ref, o_ref):
    tq = q_ref.shape[0]
    sub = math.gcd(tq, MLA_SUB)
    n = tq // sub
    k = k_ref[...]
    v = v_ref[...]
    v_ext = jnp.concatenate([v, jnp.ones_like(v)], axis=1)

    def scores(i):
        return _dot_nt(q_ref[i * sub:(i + 1) * sub, :], k)

    def finish(i, s):
        p = jnp.exp2(s - jnp.max(s, axis=-1, keepdims=True)).astype(BF16)
        ol = _dot(p, v_ext)
        o = ol[:, :D_V_MLA] / ol[:, D_V_MLA:]
        z = z_ref[i * sub:(i + 1) * sub, :].astype(F32)
        o_ref[i * sub:(i + 1) * sub, :] = (o * _silu(z)).astype(o_ref.dtype)

    s_next = scores(0)
    for i in range(n):
        s_cur = s_next
        if i + 1 < n:
            s_next = scores(i + 1)
        finish(i, s_cur)


def _mla_attn(qm, kf, v, zq, b, s, tq):
    t = b * s
    nq = s // tq
    return pl.pallas_call(
        _mla_attn_kernel,
        grid=(b, H_MLA, nq),
        in_specs=[
            pl.BlockSpec((tq, D_QK_PAD), lambda bi, h, i: (bi * nq + i, h)),
            pl.BlockSpec((s, D_QK_PAD), lambda bi, h, i: (bi, h)),
            pl.BlockSpec((s, D_V_MLA), lambda bi, h, i: (bi, h)),
            pl.BlockSpec((None, tq, LANES), lambda bi, h, i: (1, bi * nq + i, W_GDN // LANES + h)),
        ],
        out_specs=pl.BlockSpec((tq, D_V_MLA), lambda bi, h, i: (bi * nq + i, h)),
        out_shape=jax.ShapeDtypeStruct((t, W_MLA), BF16),
        compiler_params=pltpu.CompilerParams(
            dimension_semantics=("arbitrary", "arbitrary", "arbitrary"), vmem_limit_bytes=VMEM_LIMIT),
        name="mla_attn",
    )(qm, kf, v, zq)


def _gdn_masks():
    i = np.arange(CHUNK)[:, None]
    j = np.arange(CHUNK)[None, :]
    per_dir = []
    for d in range(2):
        before = (i > j) if d == 0 else (i < j)
        out = [(i == j) | before,
               before,
               before & (i // 2 == j // 2)]
        for sz in MERGE_SIZES:
            out.append(before & (i // (2 * sz) == j // (2 * sz)) & (i // sz != j // sz))
        out.append(i == j)
        per_dir.append(np.stack(out))
    return np.concatenate(per_dir, axis=2).astype(np.float32)


GDN_HEADS_PER_STEP = 2
GDN_CHUNKS_PER_ITER = 16
GDN_PREP_STAGES = 16


def _gdn_kernel(q_ref, k_ref, v_ref, z_ref, gb_ref, cwq_ref, cwk_ref, cwv_ref, gain_ref, mk_ref, o_ref,
                xpad, qs, ks, vs, gc_s, bt_s, mp_s, n_s, r_s, eg_s, st_s, o_s, *, n_steps):
    s = q_ref.shape[0]
    nc = s // CHUNK
    pad = SUBLANES
    halo = (CONV_K - 1) // 2
    group = math.gcd(nc, GDN_CHUNKS_PER_ITER)
    ngroups = nc // group
    nheads = GDN_HEADS_PER_STEP
    scans_per_group = nc // (nheads * ngroups)
    assert scans_per_group * nheads * ngroups == nc
    g = pl.program_id(0)
    wset = g % 2
    rset = 1 - wset
    head0 = (jnp.minimum(g, n_steps - 1) % (H_GDN // nheads)) * nheads

    def rows(c):
        return pl.ds(pl.multiple_of(c * CHUNK, CHUNK), CHUNK)

    def group_loop(body):
        def grouped(i, carry):
            body([i * group + t for t in range(group)], i)
            return carry
        lax.fori_loop(0, ngroups, grouped, 0)

    def chunk_loop(body):
        def each(cs, _):
            for c in cs:
                body(c)
        group_loop(each)

    @pl.when(g == 0)
    def _():
        def zero_body(c):
            for hh in range(nheads):
                for d in range(2):
                    mp_s[1, hh, d, c] = jnp.zeros(mp_s.shape[4:], BF16)
                    n_s[1, hh, d, c] = jnp.zeros(n_s.shape[4:], BF16)
                    r_s[1, hh, d, c] = jnp.zeros(r_s.shape[4:], BF16)
                    eg_s[1, hh, d, c] = jnp.zeros(eg_s.shape[4:], F32)
        chunk_loop(zero_body)

    st_s[...] = jnp.zeros(st_s.shape, F32)
    zeros = jnp.zeros((pad, LANES), F32)
    for a in range(3):
        xpad[a, 0:pad, :] = zeros
        xpad[a, pad + s:2 * pad + s, :] = zeros
    lane = lax.broadcasted_iota(jnp.int32, (CHUNK, LANES), 1)
    fwd_lanes = lane < CHUNK

    def block_diag(z):
        return jnp.concatenate([jnp.where(fwd_lanes, z, 0.0), jnp.where(fwd_lanes, 0.0, z)], axis=0).astype(BF16)

    def scan_step(j):
        for hh in range(nheads):
            for d in range(2):
                c = j if d == 0 else nc - 1 - j
                st = st_s[2 * hh + d]
                x = _dot(mp_s[rset, hh, d, c], st.astype(BF16))
                o_s[hh, d, rows(c), :] = (x[DK_GDN:] + r_s[rset, hh, d, c].astype(F32)).astype(o_s.dtype)
                st_s[2 * hh + d] = (st * eg_s[rset, hh, d, c, 0:1, :] + x[:DK_GDN]
                                    + n_s[rset, hh, d, c].astype(F32))

    for hh in range(nheads):
        head = head0 + hh
        cols = slice(hh * LANES, (hh + 1) * LANES)

        convs = ((q_ref, cwq_ref, qs, "q"), (k_ref, cwk_ref, ks, "k"), (v_ref, cwv_ref, vs, "v"))
        for a, (src, _, _, _) in enumerate(convs):
            xpad[a, pad:pad + s, :] = src[:, cols].astype(F32)
        cws = [cw_ref[:, cols] for _, cw_ref, _, _ in convs]

        def conv_body(c, head=head, cws=cws):
            base = pl.multiple_of(c * CHUNK, CHUNK)
            for a, (_, _, dst, kind) in enumerate(convs):
                cw = cws[a]
                y = cw[0:1, :] * xpad[a, pl.ds(base + pad - halo, CHUNK), :]
                for j in range(1, CONV_K):
                    y = y + cw[j:j + 1, :] * xpad[a, pl.ds(base + pad - halo + j, CHUNK), :]
                y = _silu(y)
                if kind != "v":
                    y = y * lax.rsqrt(jnp.sum(y * y, axis=-1, keepdims=True) + EPS)
                if kind == "q":
                    y = y * DK_GDN ** -0.5
                dst[pl.ds(base, CHUNK), :] = y
            gbc = gb_ref[pl.ds(base, CHUNK), :]
            for d in range(2):
                for ref, cidx in ((gc_s, d * H_GDN + head), (bt_s, (2 + d) * H_GDN + head)):
                    col = jnp.sum(jnp.where(lane == cidx, gbc, 0.0), axis=-1, keepdims=True)
                    ref[d, pl.ds(base, CHUNK), :] = jnp.broadcast_to(col, gbc.shape)

        chunk_loop(conv_body)

        def prep_group(cs, gi, hh=hh):
            n = len(cs)
            ticks = [0, 0]

            def stage_done():
                ticks[0] += 1
                while ticks[1] < scans_per_group and ticks[0] * scans_per_group >= (ticks[1] + 1) * GDN_PREP_STAGES:
                    scan_step((hh * ngroups + gi) * scans_per_group + ticks[1])
                    ticks[1] += 1

            rs = [rows(c) for c in cs]
            q = [qs[r, :] for r in rs]
            k = [ks[r, :] for r in rs]
            gc = [[gc_s[d, r, :] for d in range(2)] for r in rs]
            bt = [[bt_s[d, r, :] for d in range(2)] for r in rs]
            k16 = [t.astype(BF16) for t in k]
            qkk = [_dot_nt(jnp.concatenate([q[i].astype(BF16), k16[i]], axis=0),
                           jnp.concatenate([k16[i], k16[i]], axis=0)) for i in range(n)]
            stage_done()
            e2, l2, x2 = [], [], []
            for i in range(n):
                gcol = jnp.where(fwd_lanes, gc[i][0], gc[i][1])
                grow = jnp.transpose(jnp.concatenate(gc[i], axis=0))[:CHUNK, :]
                e = jnp.exp(jnp.where(mk_ref[MK_INCL] > 0.5, gcol - grow, -jnp.inf))
                lm = qkk[i][CHUNK:] * jnp.where(fwd_lanes, bt[i][0], bt[i][1]) * e * mk_ref[MK_STRICT]
                e2.append(e)
                l2.append(lm)
                x2.append(mk_ref[MK_EYE] - lm * mk_ref[MK_PAIR])
            stage_done()
            for t in range(len(MERGE_SIZES)):
                y2 = [_dot((l2[i] * mk_ref[MK_MERGE0 + t]).astype(BF16), block_diag(x2[i])) for i in range(n)]
                stage_done()
                x2 = [x2[i] - _dot(x2[i].astype(BF16), block_diag(y2[i])) for i in range(n)]
                stage_done()
            eg = [[jnp.exp(t) for t in gc[i]] for i in range(n)]
            gl = [[gc[i][0][CHUNK - 1:CHUNK, :], gc[i][1][0:1, :]] for i in range(n)]
            wu = []
            for i in range(n):
                v = vs[rs[i], :]
                rhs = jnp.concatenate(
                    [jnp.concatenate([(k[i] * bt[i][d] * eg[i][d]).astype(BF16), (v * bt[i][d]).astype(BF16)], axis=1)
                     for d in range(2)], axis=0)
                wu.append(_dot(block_diag(x2[i]), rhs).astype(BF16))
            stage_done()
            iwu = [_dot(block_diag(qkk[i][:CHUNK] * e2[i]), wu[i]) for i in range(n)]
            stage_done()
            kwu = [[_dot_tn((k[i] * jnp.exp(gl[i][d] - gc[i][d])).astype(BF16), wu[i][d * CHUNK:(d + 1) * CHUNK])
                    for d in range(2)] for i in range(n)]
            stage_done()
            for i, c in enumerate(cs):
                for d in range(2):
                    iw = iwu[i][d * CHUNK:(d + 1) * CHUNK]
                    mp_s[wset, hh, d, c, 0:DK_GDN, :] = (-kwu[i][d][:, :DK_GDN]).astype(BF16)
                    mp_s[wset, hh, d, c, DK_GDN:DK_GDN + CHUNK, :] = (q[i] * eg[i][d] - iw[:, :DK_GDN]).astype(BF16)
                    n_s[wset, hh, d, c] = kwu[i][d][:, DK_GDN:].astype(BF16)
                    r_s[wset, hh, d, c] = iw[:, DK_GDN:].astype(BF16)
                    eg_s[wset, hh, d, c] = jnp.broadcast_to(jnp.exp(gl[i][d]), (SUBLANES, LANES))
            stage_done()
            assert ticks == [GDN_PREP_STAGES, scans_per_group]

        group_loop(prep_group)

    gain = gain_ref[...]
    for hh in range(nheads):
        cols = slice(hh * LANES, (hh + 1) * LANES)

        def out_body(c, hh=hh, cols=cols):
            r = rows(c)
            o = o_s[hh, 0, r, :].astype(F32) + o_s[hh, 1, r, :].astype(F32)
            o_ref[r, cols] = (_rms(o, gain) * _silu(z_ref[r, cols].astype(F32))).astype(o_ref.dtype)

        chunk_loop(out_body)


def _gdn(pz, gb, conv_w, gdn_gain, masks, b, s):
    t = b * s
    nc = s // CHUNK
    nh = GDN_HEADS_PER_STEP
    w = nh * LANES
    ng = H_GDN // nh
    n_steps = b * ng
    cur = lambda g: jnp.minimum(g, n_steps - 1)
    prv = lambda g: jnp.maximum(g - 1, 0)
    col = lambda off: (lambda g: (0, cur(g) // ng, off + cur(g) % ng))
    cwcol = lambda off: (lambda g: (0, off + cur(g) % ng))
    return pl.pallas_call(
        functools.partial(_gdn_kernel, n_steps=n_steps),
        grid=(n_steps + 1,),
        in_specs=[
            pl.BlockSpec((None, s, w), col(0)),
            pl.BlockSpec((None, s, w), col(ng)),
            pl.BlockSpec((None, s, w), col(2 * ng)),
            pl.BlockSpec((None, s, w), lambda g: (1, prv(g) // ng, prv(g) % ng)),
            pl.BlockSpec((s, LANES), lambda g: (cur(g) // ng, 0)),
            pl.BlockSpec((CONV_K, w), cwcol(0)),
            pl.BlockSpec((CONV_K, w), cwcol(ng)),
            pl.BlockSpec((CONV_K, w), cwcol(2 * ng)),
            pl.BlockSpec((1, DV_GDN), lambda g: (0, 0)),
            pl.BlockSpec(masks.shape, lambda g: (0, 0, 0)),
        ],
        out_specs=pl.BlockSpec((s, w), lambda g: (prv(g) // ng, prv(g) % ng)),
        out_shape=jax.ShapeDtypeStruct((t, W_GDN), BF16),
        scratch_shapes=[
            pltpu.VMEM((3, s + 2 * SUBLANES, LANES), F32),
            pltpu.VMEM((s, LANES), F32),
            pltpu.VMEM((s, LANES), F32),
            pltpu.VMEM((s, LANES), F32),
            pltpu.VMEM((2, s, LANES), F32),
            pltpu.VMEM((2, s, LANES), F32),
            pltpu.VMEM((2, nh, 2, nc, DK_GDN + CHUNK, DV_GDN), BF16),
            pltpu.VMEM((2, nh, 2, nc, DK_GDN, DV_GDN), BF16),
            pltpu.VMEM((2, nh, 2, nc, CHUNK, DV_GDN), BF16),
            pltpu.VMEM((2, nh, 2, nc, SUBLANES, LANES), F32),
            pltpu.VMEM((2 * nh, DK_GDN, DV_GDN), F32),
            pltpu.VMEM((nh, 2, s, LANES), BF16),
        ],
        compiler_params=pltpu.CompilerParams(
            dimension_semantics=("arbitrary",), vmem_limit_bytes=VMEM_LIMIT),
        name="gdn",
    )(pz, pz, pz, pz, gb, conv_w, conv_w, conv_w, gdn_gain, masks)


def _mem_kv_kernel(m_ref, gain_ref, w_ref, o_ref):
    o_ref[...] = _dot(_rms(m_ref[...], gain_ref[...]).astype(BF16), w_ref[...]).astype(o_ref.dtype)


def _mem_kv(mem, gain, w):
    b, n, _ = mem.shape
    return pl.pallas_call(
        _mem_kv_kernel,
        grid=(b,),
        in_specs=[
            pl.BlockSpec((None, n, D_MODEL), lambda bi: (bi, 0, 0)),
            pl.BlockSpec((1, D_MODEL), lambda bi: (0, 0)),
            pl.BlockSpec((D_MODEL, 2 * W_MEM), lambda bi: (0, 0)),
        ],
        out_specs=pl.BlockSpec((None, n, 2 * W_MEM), lambda bi: (bi, 0, 0)),
        out_shape=jax.ShapeDtypeStruct((b, n, 2 * W_MEM), BF16),
        compiler_params=pltpu.CompilerParams(
            dimension_semantics=("arbitrary",), vmem_limit_bytes=VMEM_LIMIT),
        name="mem_kv",
    )(mem, gain, w)


def _out_kernel(x_ref, again_ref, wg_ref, og_ref, om_ref, zq_ref, kvm_ref, wbg_ref, wbm_ref, wbe_ref, wo_ref,
                fgain_ref, o_ref):
    x = x_ref[...]
    h = _rms(x, again_ref[...]).astype(BF16)
    gates = _sigmoid(_dot(h, wg_ref[...]))

    zq = zq_ref[...]
    kvm = kvm_ref[...]
    heads = []
    for hh in range(H_MEM):
        c0 = hh * D_MEM
        sc = _dot_nt(zq[:, W_MEM + c0:W_MEM + c0 + D_MEM], kvm[:, c0:c0 + D_MEM]) * D_MEM ** -0.5
        p = jnp.exp(sc - jnp.max(sc, axis=-1, keepdims=True))
        l = jnp.sum(p, axis=-1, keepdims=True)
        o = _dot(p.astype(BF16), kvm[:, W_MEM + c0:W_MEM + c0 + D_MEM]) / l
        heads.append((o * _silu(zq[:, c0:c0 + D_MEM].astype(F32))).astype(BF16))
    o_mem = jnp.concatenate(heads, axis=1)

    merged = (gates[:, :D_MODEL] * _dot(og_ref[...], wbg_ref[...])
              + gates[:, D_MODEL:2 * D_MODEL] * _dot(om_ref[...], wbm_ref[...])
              + gates[:, 2 * D_MODEL:] * _dot(o_mem, wbe_ref[...]))
    y = x + _dot(merged.astype(BF16), wo_ref[...])
    o_ref[...] = _rms(y, fgain_ref[...])


def _out(x2, again, wg, og, om, pz, kvm, wbg, wbm, wbe, wo, fgain, s, tm):
    t = x2.shape[0]
    nt = s // tm
    const = lambda i: (0, 0)
    row = lambda i: (i, 0)
    return pl.pallas_call(
        _out_kernel,
        grid=(t // tm,),
        in_specs=[
            pl.BlockSpec((tm, D_MODEL), row),
            pl.BlockSpec((1, D_MODEL), const),
            pl.BlockSpec((D_MODEL, N_BRANCH * D_MODEL), const),
            pl.BlockSpec((tm, W_GDN), row),
            pl.BlockSpec((tm, W_MLA), row),
            pl.BlockSpec((None, tm, 2 * W_MEM), lambda i: (1, i, (W_GDN + W_MLA) // (2 * W_MEM))),
            pl.BlockSpec((None, kvm.shape[1], 2 * W_MEM), lambda i: (i // nt, 0, 0)),
            pl.BlockSpec((W_GDN, D_MODEL), const),
            pl.BlockSpec((W_MLA, D_MODEL), const),
            pl.BlockSpec((W_MEM, D_MODEL), const),
            pl.BlockSpec((D_MODEL, D_MODEL), const),
            pl.BlockSpec((1, D_MODEL), const),
        ],
        out_specs=pl.BlockSpec((tm, D_MODEL), row),
        out_shape=jax.ShapeDtypeStruct((t, D_MODEL), F32),
        compiler_params=pltpu.CompilerParams(
            dimension_semantics=("arbitrary",), vmem_limit_bytes=VMEM_LIMIT),
        name="out",
    )(x2, again, wg, og, om, pz, kvm, wbg, wbm, wbe, wo, fgain)


def _prep_weights(attn_norm_gain, w_in, conv_w, a_log, dt_bias, gdn_norm_gain, q_norm_gain, w_q_up,
                  kv_norm_gain, w_kv_up, mem_norm_gain, w_mem_kv, w_br_gdn, w_br_mla, w_br_mem, w_out,
                  final_norm_gain):
    w = w_in[0]
    offs = np.concatenate([[0], np.cumsum(SPLIT_SIZES)])
    seg = lambda n: w[:, int(offs[n]):int(offs[n + 1])]
    w_qkv, w_ab, w_zg, w_cq, w_ckv, w_zm, w_qmem, w_zmem, w_gate = (seg(n) for n in range(9))
    half = D_ROPE // 2

    def rope_cols(wpe):
        x1, x2 = wpe[..., :half], wpe[..., half:]
        return jnp.concatenate([x1, x2, x2, x1], axis=-1)

    w_big = jnp.stack([w_qkv, jnp.concatenate([w_zg, w_zm, w_zmem, w_qmem], axis=1)]).astype(BF16)
    w_small = jnp.concatenate([
        w_cq, w_ckv[:, :KV_LORA], rope_cols(w_ckv[:, KV_LORA:]),
        w_ab, jnp.zeros((D_MODEL, LANES - 4 * H_GDN), w.dtype)], axis=1).astype(BF16)

    wq = w_q_up[0].reshape(Q_LORA, H_MLA, D_NOPE + D_ROPE)
    wq = jnp.concatenate([wq[..., :D_NOPE], rope_cols(wq[..., D_NOPE:])], axis=-1)
    wq = wq.reshape(Q_LORA, H_MLA * D_QK_PAD).astype(BF16)
    wkv = w_kv_up[0].reshape(KV_LORA, H_MLA, D_NOPE + D_V_MLA)
    wkv = jnp.concatenate([wkv[..., :D_NOPE].reshape(KV_LORA, W_MLA),
                           wkv[..., D_NOPE:].reshape(KV_LORA, W_MLA)], axis=1).astype(BF16)

    pad_lanes = lambda v: jnp.concatenate([v.reshape(1, -1), jnp.zeros((1, LANES - v.size), F32)], axis=1)
    return dict(
        again=attn_norm_gain[0].reshape(1, D_MODEL), w_big=w_big, w_small=w_small, w_gate=w_gate.astype(BF16),
        qg=q_norm_gain[0].reshape(1, Q_LORA), kvg=kv_norm_gain[0].reshape(1, KV_LORA), wq=wq, wkv=wkv,
        alog=pad_lanes(a_log[0]), dtb=pad_lanes(dt_bias[0]),
        conv_w=conv_w[0], gdn_gain=gdn_norm_gain[0].reshape(1, DV_GDN),
        mgain=mem_norm_gain[0].reshape(1, D_MODEL), w_mem_kv=w_mem_kv[0].astype(BF16),
        wbg=w_br_gdn[0].astype(BF16), wbm=w_br_mla[0].astype(BF16), wbe=w_br_mem[0].astype(BF16),
        wo=w_out[0].astype(BF16), fgain=final_norm_gain.reshape(1, D_MODEL),
        masks=jnp.asarray(_gdn_masks()),
    )


def _rope_tables(s):
    half = D_ROPE // 2
    inv_freq = ROPE_THETA ** (-jnp.arange(half, dtype=F32) / half)
    ang = jnp.arange(s, dtype=jnp.int32).astype(F32)[:, None] * inv_freq[None, :]
    cos, sin = jnp.cos(ang), jnp.sin(ang)
    tabk = jnp.concatenate([cos, cos, -sin, sin], axis=1)
    return tabk * MLA_Q_SCALE, tabk


def _tile(s, want):
    return math.gcd(s, want)


def _encode(x, mem, wts):
    b, s, _ = x.shape
    x2 = x.reshape(b * s, D_MODEL)
    tabq, tabk = _rope_tables(s)
    pz = _proj(x2, wts["again"], wts["w_big"], _tile(s, 512))
    qm, kf, v, gb = _mla_prep(x2, wts["again"], wts["w_small"], wts["qg"], wts["kvg"], wts["wq"], wts["wkv"],
                              tabq, tabk, wts["alog"], wts["dtb"], _tile(s, 256), s)
    o_mla = _mla_attn(qm, kf, v, pz, b, s, _tile(s, 2048))
    o_gdn = _gdn(pz, gb, wts["conv_w"], wts["gdn_gain"], wts["masks"], b, s)
    kvm = _mem_kv(mem, wts["mgain"], wts["w_mem_kv"])
    y = _out(x2, wts["again"], wts["w_gate"], o_gdn, o_mla, pz, kvm, wts["wbg"], wts["wbm"], wts["wbe"],
             wts["wo"], wts["fgain"], s, _tile(s, 256))
    return y.reshape(b, s, D_MODEL)


def kernel(x_prompt, x_sample, mem_prompt, mem_sample, attn_norm_gain, w_in, conv_w, a_log, dt_bias, gdn_norm_gain, q_norm_gain, w_q_up, kv_norm_gain, w_kv_up, mem_norm_gain, w_mem_kv, w_br_gdn, w_br_mla, w_br_mem, w_out, final_norm_gain):
    wts = _prep_weights(attn_norm_gain, w_in, conv_w, a_log, dt_bias, gdn_norm_gain, q_norm_gain, w_q_up,
                        kv_norm_gain, w_kv_up, mem_norm_gain, w_mem_kv, w_br_gdn, w_br_mla, w_br_mem, w_out,
                        final_norm_gain)
    return (_encode(x_prompt, mem_prompt, wts), _encode(x_sample, mem_sample, wts))
```

```python
import functools
import math

import numpy as np
import jax
import jax.numpy as jnp
from jax import lax
from jax.experimental import pallas as pl
from jax.experimental.pallas import tpu as pltpu

D_MODEL = 1024
H_GDN = 8
DK_GDN = 128
DV_GDN = 128
W_GDN = H_GDN * DV_GDN
CONV_K = 5
CHUNK = 64
H_MLA = 8
Q_LORA = 384
KV_LORA = 256
D_NOPE = 128
D_ROPE = 64
D_V_MLA = 128
W_MLA = H_MLA * D_V_MLA
ROPE_THETA = 10000.0
H_MEM = 4
D_MEM = 128
W_MEM = H_MEM * D_MEM
N_BRANCH = 3
EPS = 1e-6
W_QKV = H_GDN * (2 * DK_GDN + DV_GDN)
SPLIT_SIZES = (W_QKV, 4 * H_GDN, W_GDN, Q_LORA, KV_LORA + D_ROPE, W_MLA, W_MEM, W_MEM, N_BRANCH * D_MODEL)

LANES = 128
SUBLANES = 8
D_QK_PAD = 256
VMEM_LIMIT = 56 * 1024 * 1024

F32 = jnp.float32
BF16 = jnp.bfloat16

MK_INCL, MK_STRICT, MK_PAIR, MK_MERGE0 = 0, 1, 2, 3
MERGE_SIZES = (2, 4, 8, 16, 32)
MK_EYE = MK_MERGE0 + len(MERGE_SIZES)


def _rms(x, gain):
    return x * lax.rsqrt(jnp.mean(x * x, axis=-1, keepdims=True) + EPS) * gain


def _sigmoid(x):
    return 1.0 / (1.0 + jnp.exp2(x * -math.log2(math.e)))


def _silu(x):
    return x * _sigmoid(x)


def _dot(a, b):
    return jnp.dot(a, b, preferred_element_type=F32)


def _dot_nt(a, b):
    return lax.dot_general(a, b, (((1,), (1,)), ((), ())), preferred_element_type=F32)


def _dot_tn(a, b):
    return lax.dot_general(a, b, (((0,), (0,)), ((), ())), preferred_element_type=F32)


def _proj_kernel(x_ref, gain_ref, w_ref, o_ref):
    h = _rms(x_ref[...], gain_ref[...]).astype(BF16)
    o_ref[...] = _dot(h, w_ref[...]).astype(o_ref.dtype)


def _proj(x2, gain, w2, tm):
    t = x2.shape[0]
    ng, _, n = w2.shape
    return pl.pallas_call(
        _proj_kernel,
        grid=(ng, t // tm),
        in_specs=[
            pl.BlockSpec((tm, D_MODEL), lambda j, i: (i, 0)),
            pl.BlockSpec((1, D_MODEL), lambda j, i: (0, 0)),
            pl.BlockSpec((None, D_MODEL, n), lambda j, i: (j, 0, 0)),
        ],
        out_specs=pl.BlockSpec((None, tm, n), lambda j, i: (j, i, 0)),
        out_shape=jax.ShapeDtypeStruct((ng, t, n), BF16),
        compiler_params=pltpu.CompilerParams(
            dimension_semantics=("arbitrary", "arbitrary"), vmem_limit_bytes=VMEM_LIMIT),
        name="proj",
    )(x2, gain, w2)


W_SMALL = Q_LORA + KV_LORA + LANES + LANES
MLA_Q_SCALE = (D_NOPE + D_ROPE) ** -0.5 * math.log2(math.e)


def _mla_prep_kernel(x_ref, gain_ref, w_ref, qg_ref, kvg_ref, wq_ref, wkv_ref, tabq_ref, tabk_ref,
                     alog_ref, dtb_ref, qm_ref, kf_ref, v_ref, gb_ref):
    h = _rms(x_ref[...], gain_ref[...]).astype(BF16)
    p = _dot(h, w_ref[...])
    cq = p[:, :Q_LORA]
    kvl = p[:, Q_LORA:Q_LORA + KV_LORA]
    kpe = p[:, Q_LORA + KV_LORA:Q_LORA + KV_LORA + LANES]
    ab = p[:, Q_LORA + KV_LORA + LANES:]
    scale = MLA_Q_SCALE

    qm = _dot(_rms(cq, qg_ref[...]).astype(BF16), wq_ref[...])
    tabq = tabq_ref[...]
    for hh in range(H_MLA):
        c0 = hh * D_QK_PAD
        qm_ref[:, c0:c0 + D_NOPE] = (qm[:, c0:c0 + D_NOPE] * scale).astype(BF16)
        t = qm[:, c0 + D_NOPE:c0 + D_QK_PAD] * tabq
        qm_ref[:, c0 + D_NOPE:c0 + D_QK_PAD] = (t + pltpu.roll(t, D_ROPE, 1)).astype(BF16)

    kv = _dot(_rms(kvl, kvg_ref[...]).astype(BF16), wkv_ref[...])
    t = kpe * tabk_ref[...]
    t = t + pltpu.roll(t, D_ROPE, 1)
    lane = lax.broadcasted_iota(jnp.int32, t.shape, 1)
    kpe_rot = jnp.where(lane < D_ROPE, t, 0.0).astype(BF16)
    for hh in range(H_MLA):
        c0 = hh * D_QK_PAD
        kf_ref[:, c0:c0 + D_NOPE] = kv[:, hh * D_NOPE:(hh + 1) * D_NOPE].astype(BF16)
        kf_ref[:, c0 + D_NOPE:c0 + D_QK_PAD] = kpe_rot
    v_ref[...] = kv[:, W_MLA:].astype(BF16)

    a = ab + dtb_ref[...]
    softplus = jnp.maximum(a, 0.0) + jnp.log(1.0 + jnp.exp(-jnp.abs(a)))
    g = -jnp.exp(alog_ref[...]) * softplus
    beta = _sigmoid(ab)
    tm = g.shape[0]
    rin = lax.broadcasted_iota(jnp.int32, g.shape, 0) % CHUNK
    pre, suf = g, g
    sh = 1
    while sh < CHUNK:
        pre = pre + jnp.where(rin >= sh, pltpu.roll(pre, sh, 0), 0.0)
        suf = suf + jnp.where(rin < CHUNK - sh, pltpu.roll(suf, tm - sh, 0), 0.0)
        sh *= 2
    gb_ref[...] = jnp.where(lane < H_GDN, pre, jnp.where(lane < 2 * H_GDN, suf,
                                                          jnp.where(lane < 4 * H_GDN, beta, 0.0)))


def _mla_prep(x2, gain, w_small, qg, kvg, wq, wkv, tabq, tabk, alog, dtb, tm, s):
    t = x2.shape[0]
    npos = s // tm
    const = lambda i: (0, 0)
    row = lambda i: (i, 0)
    return pl.pallas_call(
        _mla_prep_kernel,
        grid=(t // tm,),
        in_specs=[
            pl.BlockSpec((tm, D_MODEL), row),
            pl.BlockSpec((1, D_MODEL), const),
            pl.BlockSpec((D_MODEL, W_SMALL), const),
            pl.BlockSpec((1, Q_LORA), const),
            pl.BlockSpec((1, KV_LORA), const),
            pl.BlockSpec((Q_LORA, H_MLA * D_QK_PAD), const),
            pl.BlockSpec((KV_LORA, 2 * W_MLA), const),
            pl.BlockSpec((tm, LANES), lambda i: (i % npos, 0)),
            pl.BlockSpec((tm, LANES), lambda i: (i % npos, 0)),
            pl.BlockSpec((1, LANES), const),
            pl.BlockSpec((1, LANES), const),
        ],
        out_specs=[
            pl.BlockSpec((tm, H_MLA * D_QK_PAD), row),
            pl.BlockSpec((tm, H_MLA * D_QK_PAD), row),
            pl.BlockSpec((tm, W_MLA), row),
            pl.BlockSpec((tm, LANES), row),
        ],
        out_shape=[
            jax.ShapeDtypeStruct((t, H_MLA * D_QK_PAD), BF16),
            jax.ShapeDtypeStruct((t, H_MLA * D_QK_PAD), BF16),
            jax.ShapeDtypeStruct((t, W_MLA), BF16),
            jax.ShapeDtypeStruct((t, LANES), F32),
        ],
        compiler_params=pltpu.CompilerParams(
            dimension_semantics=("arbitrary",), vmem_limit_bytes=VMEM_LIMIT),
        name="mla_prep",
    )(x2, gain, w_small, qg, kvg, wq, wkv, tabq, tabk, alog, dtb)


MLA_SUB = 256


def _mla_attn_kernel(q_ref, k_ref, v_ref, z_ref, o_ref):
    tq = q_ref.shape[0]
    sub = math.gcd(tq, MLA_SUB)
    n = tq // sub
    k = k_ref[...]
    v = v_ref[...]
    v_ext = jnp.concatenate([v, jnp.ones_like(v)], axis=1)

    def scores(i):
        return _dot_nt(q_ref[i * sub:(i + 1) * sub, :], k)

    def finish(i, s):
        p = jnp.exp2(s - jnp.max(s, axis=-1, keepdims=True)).astype(BF16)
        ol = _dot(p, v_ext)
        o = ol[:, :D_V_MLA] / ol[:, D_V_MLA:]
        z = z_ref[i * sub:(i + 1) * sub, :].astype(F32)
        o_ref[i * sub:(i + 1) * sub, :] = (o * _silu(z)).astype(o_ref.dtype)

    s_next = scores(0)
    for i in range(n):
        s_cur = s_next
        if i + 1 < n:
            s_next = scores(i + 1)
        finish(i, s_cur)


def _mla_attn(qm, kf, v, zq, b, s, tq):
    t = b * s
    nq = s // tq
    return pl.pallas_call(
        _mla_attn_kernel,
        grid=(b, H_MLA, nq),
        in_specs=[
            pl.BlockSpec((tq, D_QK_PAD), lambda bi, h, i: (bi * nq + i, h)),
            pl.BlockSpec((s, D_QK_PAD), lambda bi, h, i: (bi, h)),
            pl.BlockSpec((s, D_V_MLA), lambda bi, h, i: (bi, h)),
            pl.BlockSpec((None, tq, LANES), lambda bi, h, i: (1, bi * nq + i, W_GDN // LANES + h)),
        ],
        out_specs=pl.BlockSpec((tq, D_V_MLA), lambda bi, h, i: (bi * nq + i, h)),
        out_shape=jax.ShapeDtypeStruct((t, W_MLA), BF16),
        compiler_params=pltpu.CompilerParams(
            dimension_semantics=("arbitrary", "arbitrary", "arbitrary"), vmem_limit_bytes=VMEM_LIMIT),
        name="mla_attn",
    )(qm, kf, v, zq)


def _gdn_masks():
    i = np.arange(CHUNK)[:, None]
    j = np.arange(CHUNK)[None, :]
    per_dir = []
    for d in range(2):
        before = (i > j) if d == 0 else (i < j)
        out = [(i == j) | before,
               before,
               before & (i // 2 == j // 2)]
        for sz in MERGE_SIZES:
            out.append(before & (i // (2 * sz) == j // (2 * sz)) & (i // sz != j // sz))
        out.append(i == j)
        per_dir.append(np.stack(out))
    return np.concatenate(per_dir, axis=2).astype(np.float32)


GDN_HEADS_PER_STEP = 2
GDN_CHUNKS_PER_ITER = 16
GDN_PREP_STAGES = 16


def _gdn_kernel(q_ref, k_ref, v_ref, z_ref, gb_ref, cwq_ref, cwk_ref, cwv_ref, gain_ref, mk_ref, o_ref,
                xpad, qs, ks, vs, gc_s, bt_s, mp_s, n_s, r_s, eg_s, st_s, o_s, *, n_steps):
    s = q_ref.shape[0]
    nc = s // CHUNK
    pad = SUBLANES
    halo = (CONV_K - 1) // 2
    group = math.gcd(nc, GDN_CHUNKS_PER_ITER)
    ngroups = nc // group
    nheads = GDN_HEADS_PER_STEP
    scans_per_group = nc // (nheads * ngroups)
    assert scans_per_group * nheads * ngroups == nc
    g = pl.program_id(0)
    wset = g % 2
    rset = 1 - wset
    head0 = (jnp.minimum(g, n_steps - 1) % (H_GDN // nheads)) * nheads

    def rows(c):
        return pl.ds(pl.multiple_of(c * CHUNK, CHUNK), CHUNK)

    def group_loop(body):
        def grouped(i, carry):
            body([i * group + t for t in range(group)], i)
            return carry
        lax.fori_loop(0, ngroups, grouped, 0)

    def chunk_loop(body):
        def each(cs, _):
            for c in cs:
                body(c)
        group_loop(each)

    @pl.when(g == 0)
    def _():
        def zero_body(c):
            for hh in range(nheads):
                for d in range(2):
                    mp_s[1, hh, d, c] = jnp.zeros(mp_s.shape[4:], BF16)
                    n_s[1, hh, d, c] = jnp.zeros(n_s.shape[4:], BF16)
                    r_s[1, hh, d, c] = jnp.zeros(r_s.shape[4:], BF16)
                    eg_s[1, hh, d, c] = jnp.zeros(eg_s.shape[4:], F32)
        chunk_loop(zero_body)

    st_s[...] = jnp.zeros(st_s.shape, F32)
    zeros = jnp.zeros((pad, LANES), F32)
    for a in range(3):
        xpad[a, 0:pad, :] = zeros
        xpad[a, pad + s:2 * pad + s, :] = zeros
    lane = lax.broadcasted_iota(jnp.int32, (CHUNK, LANES), 1)
    fwd_lanes = lane < CHUNK

    def block_diag(z):
        return jnp.concatenate([jnp.where(fwd_lanes, z, 0.0), jnp.where(fwd_lanes, 0.0, z)], axis=0).astype(BF16)

    def scan_step(j):
        for hh in range(nheads):
            for d in range(2):
                c = j if d == 0 else nc - 1 - j
                st = st_s[2 * hh + d]
                x = _dot(mp_s[rset, hh, d, c], st.astype(BF16))
                o_s[hh, d, rows(c), :] = (x[DK_GDN:] + r_s[rset, hh, d, c].astype(F32)).astype(o_s.dtype)
                st_s[2 * hh + d] = (st * eg_s[rset, hh, d, c, 0:1, :] + x[:DK_GDN]
                                    + n_s[rset, hh, d, c].astype(F32))

    for hh in range(nheads):
        head = head0 + hh
        cols = slice(hh * LANES, (hh + 1) * LANES)

        convs = ((q_ref, cwq_ref, qs, "q"), (k_ref, cwk_ref, ks, "k"), (v_ref, cwv_ref, vs, "v"))
        for a, (src, _, _, _) in enumerate(convs):
            xpad[a, pad:pad + s, :] = src[:, cols].astype(F32)
        cws = [cw_ref[:, cols] for _, cw_ref, _, _ in convs]

        def conv_body(c, head=head, cws=cws):
            base = pl.multiple_of(c * CHUNK, CHUNK)
            for a, (_, _, dst, kind) in enumerate(convs):
                cw = cws[a]
                y = cw[0:1, :] * xpad[a, pl.ds(base + pad - halo, CHUNK), :]
                for j in range(1, CONV_K):
                    y = y + cw[j:j + 1, :] * xpad[a, pl.ds(base + pad - halo + j, CHUNK), :]
                y = _silu(y)
                if kind != "v":
                    inv = lax.rsqrt(jnp.sum(y * y, axis=-1, keepdims=True) + EPS)
                    y = y * (inv * DK_GDN ** -0.5 if kind == "q" else inv)
                dst[pl.ds(base, CHUNK), :] = y
            gbc = gb_ref[pl.ds(base, CHUNK), :]
            for d in range(2):
                for ref, cidx in ((gc_s, d * H_GDN + head), (bt_s, (2 + d) * H_GDN + head)):
                    col = jnp.sum(jnp.where(lane == cidx, gbc, 0.0), axis=-1, keepdims=True)
                    ref[d, pl.ds(base, CHUNK), :] = jnp.broadcast_to(col, gbc.shape)

        chunk_loop(conv_body)

        def prep_group(cs, gi, hh=hh):
            n = len(cs)
            ticks = [0, 0]

            def stage_done():
                ticks[0] += 1
                while ticks[1] < scans_per_group and ticks[0] * scans_per_group >= (ticks[1] + 1) * GDN_PREP_STAGES:
                    scan_step((hh * ngroups + gi) * scans_per_group + ticks[1])
                    ticks[1] += 1

            rs = [rows(c) for c in cs]
            q = [qs[r, :] for r in rs]
            k = [ks[r, :] for r in rs]
            gc = [[gc_s[d, r, :] for d in range(2)] for r in rs]
            bt = [[bt_s[d, r, :] for d in range(2)] for r in rs]
            k16 = [t.astype(BF16) for t in k]
            qkk = [_dot_nt(jnp.concatenate([q[i].astype(BF16), k16[i]], axis=0),
                           jnp.concatenate([k16[i], k16[i]], axis=0)) for i in range(n)]
            stage_done()
            e2, l2, x2 = [], [], []
            for i in range(n):
                gcol = jnp.where(fwd_lanes, gc[i][0], gc[i][1])
                grow = jnp.transpose(jnp.concatenate(gc[i], axis=0))[:CHUNK, :]
                e = jnp.exp(jnp.where(mk_ref[MK_INCL] > 0.5, gcol - grow, -jnp.inf))
                lm = qkk[i][CHUNK:] * jnp.where(fwd_lanes, bt[i][0], bt[i][1]) * e * mk_ref[MK_STRICT]
                e2.append(e)
                l2.append(lm)
                x2.append(mk_ref[MK_EYE] - lm * mk_ref[MK_PAIR])
            stage_done()
            for t in range(len(MERGE_SIZES)):
                y2 = [_dot((l2[i] * mk_ref[MK_MERGE0 + t]).astype(BF16), block_diag(x2[i])) for i in range(n)]
                stage_done()
                x2 = [x2[i] - _dot(x2[i].astype(BF16), block_diag(y2[i])) for i in range(n)]
                stage_done()
            eg = [[jnp.exp(t) for t in gc[i]] for i in range(n)]
            gl = [[gc[i][0][CHUNK - 1:CHUNK, :], gc[i][1][0:1, :]] for i in range(n)]
            wu = []
            for i in range(n):
                v = vs[rs[i], :]
                rhs = jnp.concatenate(
                    [jnp.concatenate([(k[i] * bt[i][d] * eg[i][d]).astype(BF16), (v * bt[i][d]).astype(BF16)], axis=1)
                     for d in range(2)], axis=0)
                wu.append(_dot(block_diag(x2[i]), rhs).astype(BF16))
            stage_done()
            iwu = [_dot(block_diag(qkk[i][:CHUNK] * e2[i]), wu[i]) for i in range(n)]
            stage_done()
            kwu = [[_dot_tn((k[i] * jnp.exp(gl[i][d] - gc[i][d])).astype(BF16), wu[i][d * CHUNK:(d + 1) * CHUNK])
                    for d in range(2)] for i in range(n)]
            stage_done()
            for i, c in enumerate(cs):
                for d in range(2):
                    iw = iwu[i][d * CHUNK:(d + 1) * CHUNK]
                    mp_s[wset, hh, d, c, 0:DK_GDN, :] = (-kwu[i][d][:, :DK_GDN]).astype(BF16)
                    mp_s[wset, hh, d, c, DK_GDN:DK_GDN + CHUNK, :] = (q[i] * eg[i][d] - iw[:, :DK_GDN]).astype(BF16)
                    n_s[wset, hh, d, c] = kwu[i][d][:, DK_GDN:].astype(BF16)
                    r_s[wset, hh, d, c] = iw[:, DK_GDN:].astype(BF16)
                    eg_s[wset, hh, d, c] = jnp.broadcast_to(jnp.exp(gl[i][d]), (SUBLANES, LANES))
            stage_done()
            assert ticks == [GDN_PREP_STAGES, scans_per_group]

        group_loop(prep_group)

    gain = gain_ref[...]
    for hh in range(nheads):
        cols = slice(hh * LANES, (hh + 1) * LANES)

        def out_body(c, hh=hh, cols=cols):
            r = rows(c)
            o = o_s[hh, 0, r, :].astype(F32) + o_s[hh, 1, r, :].astype(F32)
            o_ref[r, cols] = (_rms(o, gain) * _silu(z_ref[r, cols].astype(F32))).astype(o_ref.dtype)

        chunk_loop(out_body)


def _gdn(pz, gb, conv_w, gdn_gain, masks, b, s):
    t = b * s
    nc = s // CHUNK
    nh = GDN_HEADS_PER_STEP
    w = nh * LANES
    ng = H_GDN // nh
    n_steps = b * ng
    cur = lambda g: jnp.minimum(g, n_steps - 1)
    prv = lambda g: jnp.maximum(g - 1, 0)
    col = lambda off: (lambda g: (0, cur(g) // ng, off + cur(g) % ng))
    cwcol = lambda off: (lambda g: (0, off + cur(g) % ng))
    return pl.pallas_call(
        functools.partial(_gdn_kernel, n_steps=n_steps),
        grid=(n_steps + 1,),
        in_specs=[
            pl.BlockSpec((None, s, w), col(0)),
            pl.BlockSpec((None, s, w), col(ng)),
            pl.BlockSpec((None, s, w), col(2 * ng)),
            pl.BlockSpec((None, s, w), lambda g: (1, prv(g) // ng, prv(g) % ng)),
            pl.BlockSpec((s, LANES), lambda g: (cur(g) // ng, 0)),
            pl.BlockSpec((CONV_K, w), cwcol(0)),
            pl.BlockSpec((CONV_K, w), cwcol(ng)),
            pl.BlockSpec((CONV_K, w), cwcol(2 * ng)),
            pl.BlockSpec((1, DV_GDN), lambda g: (0, 0)),
            pl.BlockSpec(masks.shape, lambda g: (0, 0, 0)),
        ],
        out_specs=pl.BlockSpec((s, w), lambda g: (prv(g) // ng, prv(g) % ng)),
        out_shape=jax.ShapeDtypeStruct((t, W_GDN), BF16),
        scratch_shapes=[
            pltpu.VMEM((3, s + 2 * SUBLANES, LANES), F32),
            pltpu.VMEM((s, LANES), F32),
            pltpu.VMEM((s, LANES), F32),
            pltpu.VMEM((s, LANES), F32),
            pltpu.VMEM((2, s, LANES), F32),
            pltpu.VMEM((2, s, LANES), F32),
            pltpu.VMEM((2, nh, 2, nc, DK_GDN + CHUNK, DV_GDN), BF16),
            pltpu.VMEM((2, nh, 2, nc, DK_GDN, DV_GDN), BF16),
            pltpu.VMEM((2, nh, 2, nc, CHUNK, DV_GDN), BF16),
            pltpu.VMEM((2, nh, 2, nc, SUBLANES, LANES), F32),
            pltpu.VMEM((2 * nh, DK_GDN, DV_GDN), F32),
            pltpu.VMEM((nh, 2, s, LANES), BF16),
        ],
        compiler_params=pltpu.CompilerParams(
            dimension_semantics=("arbitrary",), vmem_limit_bytes=VMEM_LIMIT),
        name="gdn",
    )(pz, pz, pz, pz, gb, conv_w, conv_w, conv_w, gdn_gain, masks)


def _mem_kv_kernel(m_ref, gain_ref, w_ref, o_ref):
    o_ref[...] = _dot(_rms(m_ref[...], gain_ref[...]).astype(BF16), w_ref[...]).astype(o_ref.dtype)


def _mem_kv(mem, gain, w):
    b, n, _ = mem.shape
    return pl.pallas_call(
        _mem_kv_kernel,
        grid=(b,),
        in_specs=[
            pl.BlockSpec((None, n, D_MODEL), lambda bi: (bi, 0, 0)),
            pl.BlockSpec((1, D_MODEL), lambda bi: (0, 0)),
            pl.BlockSpec((D_MODEL, 2 * W_MEM), lambda bi: (0, 0)),
        ],
        out_specs=pl.BlockSpec((None, n, 2 * W_MEM), lambda bi: (bi, 0, 0)),
        out_shape=jax.ShapeDtypeStruct((b, n, 2 * W_MEM), BF16),
        compiler_params=pltpu.CompilerParams(
            dimension_semantics=("arbitrary",), vmem_limit_bytes=VMEM_LIMIT),
        name="mem_kv",
    )(mem, gain, w)


OUT_SUB = 256


def _out_kernel(x_ref, again_ref, wg_ref, og_ref, om_ref, zq_ref, kvm_ref, wbg_ref, wbm_ref, wbe_ref, wo_ref,
                fgain_ref, o_ref):
    tm = x_ref.shape[0]
    sub = math.gcd(tm, OUT_SUB)
    tiles = [slice(i * sub, (i + 1) * sub) for i in range(tm // sub)]
    kvm = kvm_ref[...]

    def gate_stage(r):
        x = x_ref[r, :]
        h = _rms(x, again_ref[...]).astype(BF16)
        return x, _sigmoid(_dot(h, wg_ref[...]))

    def mem_stage(r):
        zq = zq_ref[r, :]
        heads = []
        for hh in range(H_MEM):
            c0 = hh * D_MEM
            sc = _dot_nt(zq[:, W_MEM + c0:W_MEM + c0 + D_MEM], kvm[:, c0:c0 + D_MEM]) * D_MEM ** -0.5
            p = jnp.exp(sc - jnp.max(sc, axis=-1, keepdims=True))
            l = jnp.sum(p, axis=-1, keepdims=True)
            o = _dot(p.astype(BF16), kvm[:, W_MEM + c0:W_MEM + c0 + D_MEM]) / l
            heads.append((o * _silu(zq[:, c0:c0 + D_MEM].astype(F32))).astype(BF16))
        return jnp.concatenate(heads, axis=1)

    def merge_stage(r, gates, o_mem):
        merged = (gates[:, :D_MODEL] * _dot(og_ref[r, :], wbg_ref[...])
                  + gates[:, D_MODEL:2 * D_MODEL] * _dot(om_ref[r, :], wbm_ref[...])
                  + gates[:, 2 * D_MODEL:] * _dot(o_mem, wbe_ref[...]))
        return merged.astype(BF16)

    xg = [gate_stage(r) for r in tiles]
    o_mem = [mem_stage(r) for r in tiles]
    merged = [merge_stage(r, xg[i][1], o_mem[i]) for i, r in enumerate(tiles)]
    for i, r in enumerate(tiles):
        y = xg[i][0] + _dot(merged[i], wo_ref[...])
        o_ref[r, :] = _rms(y, fgain_ref[...])


def _out(x2, again, wg, og, om, pz, kvm, wbg, wbm, wbe, wo, fgain, s, tm):
    t = x2.shape[0]
    nt = s // tm
    const = lambda i: (0, 0)
    row = lambda i: (i, 0)
    return pl.pallas_call(
        _out_kernel,
        grid=(t // tm,),
        in_specs=[
            pl.BlockSpec((tm, D_MODEL), row),
            pl.BlockSpec((1, D_MODEL), const),
            pl.BlockSpec((D_MODEL, N_BRANCH * D_MODEL), const, pipeline_mode=pl.Buffered(1)),
            pl.BlockSpec((tm, W_GDN), row),
            pl.BlockSpec((tm, W_MLA), row),
            pl.BlockSpec((None, tm, 2 * W_MEM), lambda i: (1, i, (W_GDN + W_MLA) // (2 * W_MEM))),
            pl.BlockSpec((None, kvm.shape[1], 2 * W_MEM), lambda i: (i // nt, 0, 0)),
            pl.BlockSpec((W_GDN, D_MODEL), const, pipeline_mode=pl.Buffered(1)),
            pl.BlockSpec((W_MLA, D_MODEL), const, pipeline_mode=pl.Buffered(1)),
            pl.BlockSpec((W_MEM, D_MODEL), const, pipeline_mode=pl.Buffered(1)),
            pl.BlockSpec((D_MODEL, D_MODEL), const, pipeline_mode=pl.Buffered(1)),
            pl.BlockSpec((1, D_MODEL), const),
        ],
        out_specs=pl.BlockSpec((tm, D_MODEL), row),
        out_shape=jax.ShapeDtypeStruct((t, D_MODEL), F32),
        compiler_params=pltpu.CompilerParams(
            dimension_semantics=("arbitrary",), vmem_limit_bytes=VMEM_LIMIT),
        name="out",
    )(x2, again, wg, og, om, pz, kvm, wbg, wbm, wbe, wo, fgain)


def _prep_weights(attn_norm_gain, w_in, conv_w, a_log, dt_bias, gdn_norm_gain, q_norm_gain, w_q_up,
                  kv_norm_gain, w_kv_up, mem_norm_gain, w_mem_kv, w_br_gdn, w_br_mla, w_br_mem, w_out,
                  final_norm_gain):
    w = w_in[0]
    offs = np.concatenate([[0], np.cumsum(SPLIT_SIZES)])
    seg = lambda n: w[:, int(offs[n]):int(offs[n + 1])]
    w_qkv, w_ab, w_zg, w_cq, w_ckv, w_zm, w_qmem, w_zmem, w_gate = (seg(n) for n in range(9))
    half = D_ROPE // 2

    def rope_cols(wpe):
        x1, x2 = wpe[..., :half], wpe[..., half:]
        return jnp.concatenate([x1, x2, x2, x1], axis=-1)

    w_big = jnp.stack([w_qkv, jnp.concatenate([w_zg, w_zm, w_zmem, w_qmem], axis=1)]).astype(BF16)
    w_small = jnp.concatenate([
        w_cq, w_ckv[:, :KV_LORA], rope_cols(w_ckv[:, KV_LORA:]),
        w_ab, jnp.zeros((D_MODEL, LANES - 4 * H_GDN), w.dtype)], axis=1).astype(BF16)

    wq = w_q_up[0].reshape(Q_LORA, H_MLA, D_NOPE + D_ROPE)
    wq = jnp.concatenate([wq[..., :D_NOPE], rope_cols(wq[..., D_NOPE:])], axis=-1)
    wq = wq.reshape(Q_LORA, H_MLA * D_QK_PAD).astype(BF16)
    wkv = w_kv_up[0].reshape(KV_LORA, H_MLA, D_NOPE + D_V_MLA)
    wkv = jnp.concatenate([wkv[..., :D_NOPE].reshape(KV_LORA, W_MLA),
                           wkv[..., D_NOPE:].reshape(KV_LORA, W_MLA)], axis=1).astype(BF16)

    pad_lanes = lambda v: jnp.concatenate([v.reshape(1, -1), jnp.zeros((1, LANES - v.size), F32)], axis=1)
    return dict(
        again=attn_norm_gain[0].reshape(1, D_MODEL), w_big=w_big, w_small=w_small, w_gate=w_gate.astype(BF16),
        qg=q_norm_gain[0].reshape(1, Q_LORA), kvg=kv_norm_gain[0].reshape(1, KV_LORA), wq=wq, wkv=wkv,
        alog=pad_lanes(a_log[0]), dtb=pad_lanes(dt_bias[0]),
        conv_w=conv_w[0], gdn_gain=gdn_norm_gain[0].reshape(1, DV_GDN),
        mgain=mem_norm_gain[0].reshape(1, D_MODEL), w_mem_kv=w_mem_kv[0].astype(BF16),
        wbg=w_br_gdn[0].astype(BF16), wbm=w_br_mla[0].astype(BF16), wbe=w_br_mem[0].astype(BF16),
        wo=w_out[0].astype(BF16), fgain=final_norm_gain.reshape(1, D_MODEL),
        masks=jnp.asarray(_gdn_masks()),
    )


def _rope_tables(s):
    half = D_ROPE // 2
    inv_freq = ROPE_THETA ** (-jnp.arange(half, dtype=F32) / half)
    ang = jnp.arange(s, dtype=jnp.int32).astype(F32)[:, None] * inv_freq[None, :]
    cos, sin = jnp.cos(ang), jnp.sin(ang)
    tabk = jnp.concatenate([cos, cos, -sin, sin], axis=1)
    return tabk * MLA_Q_SCALE, tabk


def _tile(s, want):
    return math.gcd(s, want)


def _encode(x, mem, wts):
    b, s, _ = x.shape
    x2 = x.reshape(b * s, D_MODEL)
    tabq, tabk = _rope_tables(s)
    pz = _proj(x2, wts["again"], wts["w_big"], _tile(s, 512))
    qm, kf, v, gb = _mla_prep(x2, wts["again"], wts["w_small"], wts["qg"], wts["kvg"], wts["wq"], wts["wkv"],
                              tabq, tabk, wts["alog"], wts["dtb"], _tile(s, 512), s)
    o_mla = _mla_attn(qm, kf, v, pz, b, s, _tile(s, 2048))
    o_gdn = _gdn(pz, gb, wts["conv_w"], wts["gdn_gain"], wts["masks"], b, s)
    kvm = _mem_kv(mem, wts["mgain"], wts["w_mem_kv"])
    y = _out(x2, wts["again"], wts["w_gate"], o_gdn, o_mla, pz, kvm, wts["wbg"], wts["wbm"], wts["wbe"],
             wts["wo"], wts["fgain"], s, _tile(s, 512))
    return y.reshape(b, s, D_MODEL)


def kernel(x_prompt, x_sample, mem_prompt, mem_sample, attn_norm_gain, w_in, conv_w, a_log, dt_bias, gdn_norm_gain, q_norm_gain, w_q_up, kv_norm_gain, w_kv_up, mem_norm_gain, w_mem_kv, w_br_gdn, w_br_mla, w_br_mem, w_out, final_norm_gain):
    wts = _prep_weights(attn_norm_gain, w_in, conv_w, a_log, dt_bias, gdn_norm_gain, q_norm_gain, w_q_up,
                        kv_norm_gain, w_kv_up, mem_norm_gain, w_mem_kv, w_br_gdn, w_br_mla, w_br_mem, w_out,
                        final_norm_gain)
    return (_encode(x_prompt, mem_prompt, wts), _encode(x_sample, mem_sample, wts))
```

```python
import functools
import math

import numpy as np
import jax
import jax.numpy as jnp
from jax import lax
from jax.experimental import pallas as pl
from jax.experimental.pallas import tpu as pltpu

D_MODEL = 1024
H_GDN = 8
DK_GDN = 128
DV_GDN = 128
W_GDN = H_GDN * DV_GDN
CONV_K = 5
CHUNK = 64
H_MLA = 8
Q_LORA = 384
KV_LORA = 256
D_NOPE = 128
D_ROPE = 64
D_V_MLA = 128
W_MLA = H_MLA * D_V_MLA
ROPE_THETA = 10000.0
H_MEM = 4
D_MEM = 128
W_MEM = H_MEM * D_MEM
N_BRANCH = 3
EPS = 1e-6
W_QKV = H_GDN * (2 * DK_GDN + DV_GDN)
SPLIT_SIZES = (W_QKV, 4 * H_GDN, W_GDN, Q_LORA, KV_LORA + D_ROPE, W_MLA, W_MEM, W_MEM, N_BRANCH * D_MODEL)

LANES = 128
SUBLANES = 8
D_QK_PAD = 256
VMEM_LIMIT = 56 * 1024 * 1024

F32 = jnp.float32
BF16 = jnp.bfloat16

MK_INCL, MK_STRICT, MK_PAIR, MK_MERGE0 = 0, 1, 2, 3
MERGE_SIZES = (2, 4, 8, 16, 32)
MK_EYE = MK_MERGE0 + len(MERGE_SIZES)


def _rms(x, gain):
    return x * lax.rsqrt(jnp.mean(x * x, axis=-1, keepdims=True) + EPS) * gain


def _sigmoid(x):
    return 1.0 / (1.0 + jnp.exp2(x * -math.log2(math.e)))


def _silu(x):
    return x * _sigmoid(x)


def _dot(a, b):
    return jnp.dot(a, b, preferred_element_type=F32)


def _dot_nt(a, b):
    return lax.dot_general(a, b, (((1,), (1,)), ((), ())), preferred_element_type=F32)


def _dot_tn(a, b):
    return lax.dot_general(a, b, (((0,), (0,)), ((), ())), preferred_element_type=F32)


def _proj_kernel(x_ref, gain_ref, w_ref, o_ref):
    h = _rms(x_ref[...], gain_ref[...]).astype(BF16)
    o_ref[...] = _dot(h, w_ref[...]).astype(o_ref.dtype)


def _proj(x2, gain, w2, tm):
    t = x2.shape[0]
    ng, _, n = w2.shape
    return pl.pallas_call(
        _proj_kernel,
        grid=(ng, t // tm),
        in_specs=[
            pl.BlockSpec((tm, D_MODEL), lambda j, i: (i, 0)),
            pl.BlockSpec((1, D_MODEL), lambda j, i: (0, 0)),
            pl.BlockSpec((None, D_MODEL, n), lambda j, i: (j, 0, 0)),
        ],
        out_specs=pl.BlockSpec((None, tm, n), lambda j, i: (j, i, 0)),
        out_shape=jax.ShapeDtypeStruct((ng, t, n), BF16),
        compiler_params=pltpu.CompilerParams(
            dimension_semantics=("arbitrary", "arbitrary"), vmem_limit_bytes=VMEM_LIMIT),
        name="proj",
    )(x2, gain, w2)


W_SMALL = Q_LORA + KV_LORA + LANES + LANES
MLA_Q_SCALE = (D_NOPE + D_ROPE) ** -0.5 * math.log2(math.e)


def _mla_prep_kernel(x_ref, gain_ref, w_ref, qg_ref, kvg_ref, wq_ref, wkv_ref, tabq_ref, tabk_ref,
                     alog_ref, dtb_ref, qm_ref, kf_ref, v_ref, gb_ref):
    h = _rms(x_ref[...], gain_ref[...]).astype(BF16)
    p = _dot(h, w_ref[...])
    cq = p[:, :Q_LORA]
    kvl = p[:, Q_LORA:Q_LORA + KV_LORA]
    kpe = p[:, Q_LORA + KV_LORA:Q_LORA + KV_LORA + LANES]
    ab = p[:, Q_LORA + KV_LORA + LANES:]
    scale = MLA_Q_SCALE

    qm = _dot(_rms(cq, qg_ref[...]).astype(BF16), wq_ref[...])
    tabq = tabq_ref[...]
    for hh in range(H_MLA):
        c0 = hh * D_QK_PAD
        qm_ref[:, c0:c0 + D_NOPE] = (qm[:, c0:c0 + D_NOPE] * scale).astype(BF16)
        t = qm[:, c0 + D_NOPE:c0 + D_QK_PAD] * tabq
        qm_ref[:, c0 + D_NOPE:c0 + D_QK_PAD] = (t + pltpu.roll(t, D_ROPE, 1)).astype(BF16)

    kv = _dot(_rms(kvl, kvg_ref[...]).astype(BF16), wkv_ref[...])
    t = kpe * tabk_ref[...]
    t = t + pltpu.roll(t, D_ROPE, 1)
    lane = lax.broadcasted_iota(jnp.int32, t.shape, 1)
    kpe_rot = jnp.where(lane < D_ROPE, t, 0.0).astype(BF16)
    for hh in range(H_MLA):
        c0 = hh * D_QK_PAD
        kf_ref[:, c0:c0 + D_NOPE] = kv[:, hh * D_NOPE:(hh + 1) * D_NOPE].astype(BF16)
        kf_ref[:, c0 + D_NOPE:c0 + D_QK_PAD] = kpe_rot
    v_ref[...] = kv[:, W_MLA:].astype(BF16)

    a = ab + dtb_ref[...]
    softplus = jnp.maximum(a, 0.0) + jnp.log(1.0 + jnp.exp(-jnp.abs(a)))
    g = -jnp.exp(alog_ref[...]) * softplus
    beta = _sigmoid(ab)
    tm = g.shape[0]
    rin = lax.broadcasted_iota(jnp.int32, g.shape, 0) % CHUNK
    pre, suf = g, g
    sh = 1
    while sh < CHUNK:
        pre = pre + jnp.where(rin >= sh, pltpu.roll(pre, sh, 0), 0.0)
        suf = suf + jnp.where(rin < CHUNK - sh, pltpu.roll(suf, tm - sh, 0), 0.0)
        sh *= 2
    gb_ref[...] = jnp.where(lane < H_GDN, pre, jnp.where(lane < 2 * H_GDN, suf,
                                                          jnp.where(lane < 4 * H_GDN, beta, 0.0)))


def _mla_prep(x2, gain, w_small, qg, kvg, wq, wkv, tabq, tabk, alog, dtb, tm, s):
    t = x2.shape[0]
    npos = s // tm
    const = lambda i: (0, 0)
    row = lambda i: (i, 0)
    return pl.pallas_call(
        _mla_prep_kernel,
        grid=(t // tm,),
        in_specs=[
            pl.BlockSpec((tm, D_MODEL), row),
            pl.BlockSpec((1, D_MODEL), const),
            pl.BlockSpec((D_MODEL, W_SMALL), const),
            pl.BlockSpec((1, Q_LORA), const),
            pl.BlockSpec((1, KV_LORA), const),
            pl.BlockSpec((Q_LORA, H_MLA * D_QK_PAD), const),
            pl.BlockSpec((KV_LORA, 2 * W_MLA), const),
            pl.BlockSpec((tm, LANES), lambda i: (i % npos, 0)),
            pl.BlockSpec((tm, LANES), lambda i: (i % npos, 0)),
            pl.BlockSpec((1, LANES), const),
            pl.BlockSpec((1, LANES), const),
        ],
        out_specs=[
            pl.BlockSpec((tm, H_MLA * D_QK_PAD), row),
            pl.BlockSpec((tm, H_MLA * D_QK_PAD), row),
            pl.BlockSpec((tm, W_MLA), row),
            pl.BlockSpec((tm, LANES), row),
        ],
        out_shape=[
            jax.ShapeDtypeStruct((t, H_MLA * D_QK_PAD), BF16),
            jax.ShapeDtypeStruct((t, H_MLA * D_QK_PAD), BF16),
            jax.ShapeDtypeStruct((t, W_MLA), BF16),
            jax.ShapeDtypeStruct((t, LANES), F32),
        ],
        compiler_params=pltpu.CompilerParams(
            dimension_semantics=("arbitrary",), vmem_limit_bytes=VMEM_LIMIT),
        name="mla_prep",
    )(x2, gain, w_small, qg, kvg, wq, wkv, tabq, tabk, alog, dtb)


MLA_SUB = 256


def _mla_attn_kernel(q_ref, k_ref, v_ref, z_ref, o_ref):
    tq = q_ref.shape[0]
    sub = math.gcd(tq, MLA_SUB)
    n = tq // sub
    k = k_ref[...]
    v = v_ref[...]
    v_ext = jnp.concatenate([v, jnp.ones_like(v)], axis=1)

    def scores(i):
        return _dot_nt(q_ref[i * sub:(i + 1) * sub, :], k)

    def finish(i, s):
        p = jnp.exp2(s - jnp.max(s, axis=-1, keepdims=True)).astype(BF16)
        ol = _dot(p, v_ext)
        o = ol[:, :D_V_MLA] / ol[:, D_V_MLA:]
        z = z_ref[i * sub:(i + 1) * sub, :].astype(F32)
        o_ref[i * sub:(i + 1) * sub, :] = (o * _silu(z)).astype(o_ref.dtype)

    s_next = scores(0)
    for i in range(n):
        s_cur = s_next
        if i + 1 < n:
            s_next = scores(i + 1)
        finish(i, s_cur)


def _mla_attn(qm, kf, v, zq, b, s, tq):
    t = b * s
    nq = s // tq
    return pl.pallas_call(
        _mla_attn_kernel,
        grid=(b, H_MLA, nq),
        in_specs=[
            pl.BlockSpec((tq, D_QK_PAD), lambda bi, h, i: (bi * nq + i, h)),
            pl.BlockSpec((s, D_QK_PAD), lambda bi, h, i: (bi, h)),
            pl.BlockSpec((s, D_V_MLA), lambda bi, h, i: (bi, h)),
            pl.BlockSpec((None, tq, LANES), lambda bi, h, i: (1, bi * nq + i, W_GDN // LANES + h)),
        ],
        out_specs=pl.BlockSpec((tq, D_V_MLA), lambda bi, h, i: (bi * nq + i, h)),
        out_shape=jax.ShapeDtypeStruct((t, W_MLA), BF16),
        compiler_params=pltpu.CompilerParams(
            dimension_semantics=("arbitrary", "arbitrary", "arbitrary"), vmem_limit_bytes=VMEM_LIMIT),
        name="mla_attn",
    )(qm, kf, v, zq)


def _gdn_masks():
    i = np.arange(CHUNK)[:, None]
    j = np.arange(CHUNK)[None, :]
    per_dir = []
    for d in range(2):
        before = (i > j) if d == 0 else (i < j)
        out = [(i == j) | before,
               before,
               before & (i // 2 == j // 2)]
        for sz in MERGE_SIZES:
            out.append(before & (i // (2 * sz) == j // (2 * sz)) & (i // sz != j // sz))
        out.append(i == j)
        per_dir.append(np.stack(out))
    return np.concatenate(per_dir, axis=2).astype(np.float32)


GDN_HEADS_PER_STEP = 2
GDN_CHUNKS_PER_ITER = 16
GDN_PREP_STAGES = 16


def _gdn_kernel(q_ref, k_ref, v_ref, z_ref, gb_ref, cwq_ref, cwk_ref, cwv_ref, gain_ref, mk_ref, o_ref,
                xpad, qs, ks, vs, mp_s, n_s, r_s, eg_s, st_s, o_s, *, n_steps):
    s = q_ref.shape[0]
    nc = s // CHUNK
    pad = SUBLANES
    halo = (CONV_K - 1) // 2
    group = math.gcd(nc, GDN_CHUNKS_PER_ITER)
    ngroups = nc // group
    nheads = GDN_HEADS_PER_STEP
    scans_per_group = nc // (nheads * ngroups)
    assert scans_per_group * nheads * ngroups == nc and group <= GDN_PREP_STAGES
    g = pl.program_id(0)
    wset = g % 2
    rset = 1 - wset
    head0 = (jnp.minimum(g, n_steps - 1) % (H_GDN // nheads)) * nheads

    def rows(c):
        return pl.ds(pl.multiple_of(c * CHUNK, CHUNK), CHUNK)

    def group_loop(body):
        def grouped(i, carry):
            body([i * group + t for t in range(group)], i)
            return carry
        lax.fori_loop(0, ngroups, grouped, 0)

    def chunk_loop(body):
        def each(cs, _):
            for c in cs:
                body(c)
        group_loop(each)

    @pl.when(g == 0)
    def _():
        def zero_body(c):
            for hh in range(nheads):
                for d in range(2):
                    mp_s[1, hh, d, c] = jnp.zeros(mp_s.shape[4:], BF16)
                    n_s[1, hh, d, c] = jnp.zeros(n_s.shape[4:], BF16)
                    r_s[1, hh, d, c] = jnp.zeros(r_s.shape[4:], BF16)
                    eg_s[1, hh, d, c] = jnp.zeros(eg_s.shape[4:], F32)
        chunk_loop(zero_body)

    st_s[...] = jnp.zeros(st_s.shape, F32)
    zeros = jnp.zeros((pad, LANES), F32)
    for a in range(3):
        xpad[a, 0:pad, :] = zeros
        xpad[a, pad + s:2 * pad + s, :] = zeros
    lane = lax.broadcasted_iota(jnp.int32, (CHUNK, LANES), 1)
    fwd_lanes = lane < CHUNK

    def block_diag(z):
        return jnp.concatenate([jnp.where(fwd_lanes, z, 0.0), jnp.where(fwd_lanes, 0.0, z)], axis=0).astype(BF16)

    def scan_step(j):
        for hh in range(nheads):
            for d in range(2):
                c = j if d == 0 else nc - 1 - j
                st = st_s[2 * hh + d]
                x = _dot(mp_s[rset, hh, d, c], st.astype(BF16))
                o_s[hh, d, rows(c), :] = (x[DK_GDN:] + r_s[rset, hh, d, c].astype(F32)).astype(o_s.dtype)
                st_s[2 * hh + d] = (st * eg_s[rset, hh, d, c, 0:1, :] + x[:DK_GDN]
                                    + n_s[rset, hh, d, c].astype(F32))

    convs = ((q_ref, cwq_ref, qs, "q"), (k_ref, cwk_ref, ks, "k"), (v_ref, cwv_ref, vs, "v"))

    def fill_xpad(hh):
        for a, (src, _, _, _) in enumerate(convs):
            xpad[a, pad:pad + s, :] = src[:, hh * LANES:(hh + 1) * LANES].astype(F32)

    def conv_chunk(hh, c):
        base = pl.multiple_of(c * CHUNK, CHUNK)
        for a, (_, cw_ref, dst, kind) in enumerate(convs):
            cw = cw_ref[:, hh * LANES:(hh + 1) * LANES]
            y = cw[0:1, :] * xpad[a, pl.ds(base + pad - halo, CHUNK), :]
            for j in range(1, CONV_K):
                y = y + cw[j:j + 1, :] * xpad[a, pl.ds(base + pad - halo + j, CHUNK), :]
            y = _silu(y)
            if kind != "v":
                inv = lax.rsqrt(jnp.sum(y * y, axis=-1, keepdims=True) + EPS)
                y = y * (inv * DK_GDN ** -0.5 if kind == "q" else inv)
            dst[hh, pl.ds(base, CHUNK), :] = y

    fill_xpad(0)
    chunk_loop(functools.partial(conv_chunk, 0))

    for hh in range(nheads):
        head = head0 + hh
        if hh + 1 < nheads:
            fill_xpad(hh + 1)

        def prep_group(cs, gi, hh=hh, head=head):
            n = len(cs)
            ticks = [0, 0]

            def stage_done():
                if hh + 1 < nheads and ticks[0] < n:
                    conv_chunk(hh + 1, cs[ticks[0]])
                ticks[0] += 1
                while ticks[1] < scans_per_group and ticks[0] * scans_per_group >= (ticks[1] + 1) * GDN_PREP_STAGES:
                    scan_step((hh * ngroups + gi) * scans_per_group + ticks[1])
                    ticks[1] += 1

            def gate_columns(r):
                gbc = gb_ref[r, :]
                out = []
                for cidx in (head, H_GDN + head, 2 * H_GDN + head, 3 * H_GDN + head):
                    col = jnp.sum(jnp.where(lane == cidx, gbc, 0.0), axis=-1, keepdims=True)
                    out.append(jnp.broadcast_to(col, gbc.shape))
                return out

            rs = [rows(c) for c in cs]
            q = [qs[hh, r, :] for r in rs]
            k = [ks[hh, r, :] for r in rs]
            cols4 = [gate_columns(r) for r in rs]
            gc = [c4[0:2] for c4 in cols4]
            bt = [c4[2:4] for c4 in cols4]
            k16 = [t.astype(BF16) for t in k]
            qkk = [_dot_nt(jnp.concatenate([q[i].astype(BF16), k16[i]], axis=0),
                           jnp.concatenate([k16[i], k16[i]], axis=0)) for i in range(n)]
            stage_done()
            e2, l2, x2 = [], [], []
            for i in range(n):
                gcol = jnp.where(fwd_lanes, gc[i][0], gc[i][1])
                grow = jnp.transpose(jnp.concatenate(gc[i], axis=0))[:CHUNK, :]
                e = jnp.exp(jnp.where(mk_ref[MK_INCL] > 0.5, gcol - grow, -jnp.inf))
                lm = qkk[i][CHUNK:] * jnp.where(fwd_lanes, bt[i][0], bt[i][1]) * e * mk_ref[MK_STRICT]
                e2.append(e)
                l2.append(lm)
                x2.append(mk_ref[MK_EYE] - lm * mk_ref[MK_PAIR])
            stage_done()
            for t in range(len(MERGE_SIZES)):
                y2 = [_dot((l2[i] * mk_ref[MK_MERGE0 + t]).astype(BF16), block_diag(x2[i])) for i in range(n)]
                stage_done()
                x2 = [x2[i] - _dot(x2[i].astype(BF16), block_diag(y2[i])) for i in range(n)]
                stage_done()
            eg = [[jnp.exp(t) for t in gc[i]] for i in range(n)]
            gl = [[gc[i][0][CHUNK - 1:CHUNK, :], gc[i][1][0:1, :]] for i in range(n)]
            wu = []
            for i in range(n):
                v = vs[hh, rs[i], :]
                rhs = jnp.concatenate(
                    [jnp.concatenate([(k[i] * bt[i][d] * eg[i][d]).astype(BF16), (v * bt[i][d]).astype(BF16)], axis=1)
                     for d in range(2)], axis=0)
                wu.append(_dot(block_diag(x2[i]), rhs).astype(BF16))
            stage_done()
            iwu = [_dot(block_diag(qkk[i][:CHUNK] * e2[i]), wu[i]) for i in range(n)]
            stage_done()
            kwu = [[_dot_tn((k[i] * jnp.exp(gl[i][d] - gc[i][d])).astype(BF16), wu[i][d * CHUNK:(d + 1) * CHUNK])
                    for d in range(2)] for i in range(n)]
            stage_done()
            for i, c in enumerate(cs):
                for d in range(2):
                    iw = iwu[i][d * CHUNK:(d + 1) * CHUNK]
                    mp_s[wset, hh, d, c, 0:DK_GDN, :] = (-kwu[i][d][:, :DK_GDN]).astype(BF16)
                    mp_s[wset, hh, d, c, DK_GDN:DK_GDN + CHUNK, :] = (q[i] * eg[i][d] - iw[:, :DK_GDN]).astype(BF16)
                    n_s[wset, hh, d, c] = kwu[i][d][:, DK_GDN:].astype(BF16)
                    r_s[wset, hh, d, c] = iw[:, DK_GDN:].astype(BF16)
                    eg_s[wset, hh, d, c] = jnp.broadcast_to(jnp.exp(gl[i][d]), (SUBLANES, LANES))
            stage_done()
            assert ticks == [GDN_PREP_STAGES, scans_per_group]

        group_loop(prep_group)

    gain = gain_ref[...]
    for hh in range(nheads):
        cols = slice(hh * LANES, (hh + 1) * LANES)

        def out_body(c, hh=hh, cols=cols):
            r = rows(c)
            o = o_s[hh, 0, r, :].astype(F32) + o_s[hh, 1, r, :].astype(F32)
            o_ref[r, cols] = (_rms(o, gain) * _silu(z_ref[r, cols].astype(F32))).astype(o_ref.dtype)

        chunk_loop(out_body)


def _gdn(pz, gb, conv_w, gdn_gain, masks, b, s):
    t = b * s
    nc = s // CHUNK
    nh = GDN_HEADS_PER_STEP
    w = nh * LANES
    ng = H_GDN // nh
    n_steps = b * ng
    cur = lambda g: jnp.minimum(g, n_steps - 1)
    prv = lambda g: jnp.maximum(g - 1, 0)
    col = lambda off: (lambda g: (0, cur(g) // ng, off + cur(g) % ng))
    cwcol = lambda off: (lambda g: (0, off + cur(g) % ng))
    return pl.pallas_call(
        functools.partial(_gdn_kernel, n_steps=n_steps),
        grid=(n_steps + 1,),
        in_specs=[
            pl.BlockSpec((None, s, w), col(0)),
            pl.BlockSpec((None, s, w), col(ng)),
            pl.BlockSpec((None, s, w), col(2 * ng)),
            pl.BlockSpec((None, s, w), lambda g: (1, prv(g) // ng, prv(g) % ng)),
            pl.BlockSpec((s, LANES), lambda g: (cur(g) // ng, 0)),
            pl.BlockSpec((CONV_K, w), cwcol(0)),
            pl.BlockSpec((CONV_K, w), cwcol(ng)),
            pl.BlockSpec((CONV_K, w), cwcol(2 * ng)),
            pl.BlockSpec((1, DV_GDN), lambda g: (0, 0)),
            pl.BlockSpec(masks.shape, lambda g: (0, 0, 0)),
        ],
        out_specs=pl.BlockSpec((s, w), lambda g: (prv(g) // ng, prv(g) % ng)),
        out_shape=jax.ShapeDtypeStruct((t, W_GDN), BF16),
        scratch_shapes=[
            pltpu.VMEM((3, s + 2 * SUBLANES, LANES), F32),
            pltpu.VMEM((nh, s, LANES), F32),
            pltpu.VMEM((nh, s, LANES), F32),
            pltpu.VMEM((nh, s, LANES), F32),
            pltpu.VMEM((2, nh, 2, nc, DK_GDN + CHUNK, DV_GDN), BF16),
            pltpu.VMEM((2, nh, 2, nc, DK_GDN, DV_GDN), BF16),
            pltpu.VMEM((2, nh, 2, nc, CHUNK, DV_GDN), BF16),
            pltpu.VMEM((2, nh, 2, nc, SUBLANES, LANES), F32),
            pltpu.VMEM((2 * nh, DK_GDN, DV_GDN), F32),
            pltpu.VMEM((nh, 2, s, LANES), BF16),
        ],
        compiler_params=pltpu.CompilerParams(
            dimension_semantics=("arbitrary",), vmem_limit_bytes=VMEM_LIMIT),
        name="gdn",
    )(pz, pz, pz, pz, gb, conv_w, conv_w, conv_w, gdn_gain, masks)


def _mem_kv_kernel(m_ref, gain_ref, w_ref, o_ref):
    o_ref[...] = _dot(_rms(m_ref[...], gain_ref[...]).astype(BF16), w_ref[...]).astype(o_ref.dtype)


def _mem_kv(mem, gain, w):
    b, n, _ = mem.shape
    return pl.pallas_call(
        _mem_kv_kernel,
        grid=(b,),
        in_specs=[
            pl.BlockSpec((None, n, D_MODEL), lambda bi: (bi, 0, 0)),
            pl.BlockSpec((1, D_MODEL), lambda bi: (0, 0)),
            pl.BlockSpec((D_MODEL, 2 * W_MEM), lambda bi: (0, 0)),
        ],
        out_specs=pl.BlockSpec((None, n, 2 * W_MEM), lambda bi: (bi, 0, 0)),
        out_shape=jax.ShapeDtypeStruct((b, n, 2 * W_MEM), BF16),
        compiler_params=pltpu.CompilerParams(
            dimension_semantics=("arbitrary",), vmem_limit_bytes=VMEM_LIMIT),
        name="mem_kv",
    )(mem, gain, w)


OUT_SUB = 256


def _out_kernel(x_ref, again_ref, wg_ref, og_ref, om_ref, zq_ref, kvm_ref, wbg_ref, wbm_ref, wbe_ref, wo_ref,
                fgain_ref, o_ref):
    tm = x_ref.shape[0]
    sub = math.gcd(tm, OUT_SUB)
    tiles = [slice(i * sub, (i + 1) * sub) for i in range(tm // sub)]
    kvm = kvm_ref[...]

    def gate_stage(r):
        x = x_ref[r, :]
        h = _rms(x, again_ref[...]).astype(BF16)
        return x, _sigmoid(_dot(h, wg_ref[...]))

    def mem_stage(r):
        zq = zq_ref[r, :]
        heads = []
        for hh in range(H_MEM):
            c0 = hh * D_MEM
            sc = _dot_nt(zq[:, W_MEM + c0:W_MEM + c0 + D_MEM], kvm[:, c0:c0 + D_MEM]) * D_MEM ** -0.5
            p = jnp.exp(sc - jnp.max(sc, axis=-1, keepdims=True))
            l = jnp.sum(p, axis=-1, keepdims=True)
            o = _dot(p.astype(BF16), kvm[:, W_MEM + c0:W_MEM + c0 + D_MEM]) / l
            heads.append((o * _silu(zq[:, c0:c0 + D_MEM].astype(F32))).astype(BF16))
        return jnp.concatenate(heads, axis=1)

    def merge_stage(r, gates, o_mem):
        merged = (gates[:, :D_MODEL] * _dot(og_ref[r, :], wbg_ref[...])
                  + gates[:, D_MODEL:2 * D_MODEL] * _dot(om_ref[r, :], wbm_ref[...])
                  + gates[:, 2 * D_MODEL:] * _dot(o_mem, wbe_ref[...]))
        return merged.astype(BF16)

    xg = [gate_stage(r) for r in tiles]
    o_mem = [mem_stage(r) for r in tiles]
    merged = [merge_stage(r, xg[i][1], o_mem[i]) for i, r in enumerate(tiles)]
    for i, r in enumerate(tiles):
        y = xg[i][0] + _dot(merged[i], wo_ref[...])
        o_ref[r, :] = _rms(y, fgain_ref[...])


def _out(x2, again, wg, og, om, pz, kvm, wbg, wbm, wbe, wo, fgain, s, tm):
    t = x2.shape[0]
    nt = s // tm
    const = lambda i: (0, 0)
    row = lambda i: (i, 0)
    return pl.pallas_call(
        _out_kernel,
        grid=(t // tm,),
        in_specs=[
            pl.BlockSpec((tm, D_MODEL), row),
            pl.BlockSpec((1, D_MODEL), const),
            pl.BlockSpec((D_MODEL, N_BRANCH * D_MODEL), const, pipeline_mode=pl.Buffered(1)),
            pl.BlockSpec((tm, W_GDN), row),
            pl.BlockSpec((tm, W_MLA), row),
            pl.BlockSpec((None, tm, 2 * W_MEM), lambda i: (1, i, (W_GDN + W_MLA) // (2 * W_MEM))),
            pl.BlockSpec((None, kvm.shape[1], 2 * W_MEM), lambda i: (i // nt, 0, 0)),
            pl.BlockSpec((W_GDN, D_MODEL), const, pipeline_mode=pl.Buffered(1)),
            pl.BlockSpec((W_MLA, D_MODEL), const, pipeline_mode=pl.Buffered(1)),
            pl.BlockSpec((W_MEM, D_MODEL), const, pipeline_mode=pl.Buffered(1)),
            pl.BlockSpec((D_MODEL, D_MODEL), const, pipeline_mode=pl.Buffered(1)),
            pl.BlockSpec((1, D_MODEL), const),
        ],
        out_specs=pl.BlockSpec((tm, D_MODEL), row),
        out_shape=jax.ShapeDtypeStruct((t, D_MODEL), F32),
        compiler_params=pltpu.CompilerParams(
            dimension_semantics=("arbitrary",), vmem_limit_bytes=VMEM_LIMIT),
        name="out",
    )(x2, again, wg, og, om, pz, kvm, wbg, wbm, wbe, wo, fgain)


def _prep_weights(attn_norm_gain, w_in, conv_w, a_log, dt_bias, gdn_norm_gain, q_norm_gain, w_q_up,
                  kv_norm_gain, w_kv_up, mem_norm_gain, w_mem_kv, w_br_gdn, w_br_mla, w_br_mem, w_out,
                  final_norm_gain):
    w = w_in[0]
    offs = np.concatenate([[0], np.cumsum(SPLIT_SIZES)])
    seg = lambda n: w[:, int(offs[n]):int(offs[n + 1])]
    w_qkv, w_ab, w_zg, w_cq, w_ckv, w_zm, w_qmem, w_zmem, w_gate = (seg(n) for n in range(9))
    half = D_ROPE // 2

    def rope_cols(wpe):
        x1, x2 = wpe[..., :half], wpe[..., half:]
        return jnp.concatenate([x1, x2, x2, x1], axis=-1)

    w_big = jnp.stack([w_qkv, jnp.concatenate([w_zg, w_zm, w_zmem, w_qmem], axis=1)]).astype(BF16)
    w_small = jnp.concatenate([
        w_cq, w_ckv[:, :KV_LORA], rope_cols(w_ckv[:, KV_LORA:]),
        w_ab, jnp.zeros((D_MODEL, LANES - 4 * H_GDN), w.dtype)], axis=1).astype(BF16)

    wq = w_q_up[0].reshape(Q_LORA, H_MLA, D_NOPE + D_ROPE)
    wq = jnp.concatenate([wq[..., :D_NOPE], rope_cols(wq[..., D_NOPE:])], axis=-1)
    wq = wq.reshape(Q_LORA, H_MLA * D_QK_PAD).astype(BF16)
    wkv = w_kv_up[0].reshape(KV_LORA, H_MLA, D_NOPE + D_V_MLA)
    wkv = jnp.concatenate([wkv[..., :D_NOPE].reshape(KV_LORA, W_MLA),
                           wkv[..., D_NOPE:].reshape(KV_LORA, W_MLA)], axis=1).astype(BF16)

    pad_lanes = lambda v: jnp.concatenate([v.reshape(1, -1), jnp.zeros((1, LANES - v.size), F32)], axis=1)
    return dict(
        again=attn_norm_gain[0].reshape(1, D_MODEL), w_big=w_big, w_small=w_small, w_gate=w_gate.astype(BF16),
        qg=q_norm_gain[0].reshape(1, Q_LORA), kvg=kv_norm_gain[0].reshape(1, KV_LORA), wq=wq, wkv=wkv,
        alog=pad_lanes(a_log[0]), dtb=pad_lanes(dt_bias[0]),
        conv_w=conv_w[0], gdn_gain=gdn_norm_gain[0].reshape(1, DV_GDN),
        mgain=mem_norm_gain[0].reshape(1, D_MODEL), w_mem_kv=w_mem_kv[0].astype(BF16),
        wbg=w_br_gdn[0].astype(BF16), wbm=w_br_mla[0].astype(BF16), wbe=w_br_mem[0].astype(BF16),
        wo=w_out[0].astype(BF16), fgain=final_norm_gain.reshape(1, D_MODEL),
        masks=jnp.asarray(_gdn_masks()),
    )


def _rope_tables(s):
    half = D_ROPE // 2
    inv_freq = ROPE_THETA ** (-jnp.arange(half, dtype=F32) / half)
    ang = jnp.arange(s, dtype=jnp.int32).astype(F32)[:, None] * inv_freq[None, :]
    cos, sin = jnp.cos(ang), jnp.sin(ang)
    tabk = jnp.concatenate([cos, cos, -sin, sin], axis=1)
    return tabk * MLA_Q_SCALE, tabk


def _tile(s, want):
    return math.gcd(s, want)


def _encode(x, mem, wts):
    b, s, _ = x.shape
    x2 = x.reshape(b * s, D_MODEL)
    tabq, tabk = _rope_tables(s)
    pz = _proj(x2, wts["again"], wts["w_big"], _tile(s, 512))
    qm, kf, v, gb = _mla_prep(x2, wts["again"], wts["w_small"], wts["qg"], wts["kvg"], wts["wq"], wts["wkv"],
                              tabq, tabk, wts["alog"], wts["dtb"], _tile(s, 512), s)
    o_mla = _mla_attn(qm, kf, v, pz, b, s, _tile(s, 2048))
    o_gdn = _gdn(pz, gb, wts["conv_w"], wts["gdn_gain"], wts["masks"], b, s)
    kvm = _mem_kv(mem, wts["mgain"], wts["w_mem_kv"])
    y = _out(x2, wts["again"], wts["w_gate"], o_gdn, o_mla, pz, kvm, wts["wbg"], wts["wbm"], wts["wbe"],
             wts["wo"], wts["fgain"], s, _tile(s, 512))
    return y.reshape(b, s, D_MODEL)


def kernel(x_prompt, x_sample, mem_prompt, mem_sample, attn_norm_gain, w_in, conv_w, a_log, dt_bias, gdn_norm_gain, q_norm_gain, w_q_up, kv_norm_gain, w_kv_up, mem_norm_gain, w_mem_kv, w_br_gdn, w_br_mla, w_br_mem, w_out, final_norm_gain):
    wts = _prep_weights(attn_norm_gain, w_in, conv_w, a_log, dt_bias, gdn_norm_gain, q_norm_gain, w_q_up,
                        kv_norm_gain, w_kv_up, mem_norm_gain, w_mem_kv, w_br_gdn, w_br_mla, w_br_mem, w_out,
                        final_norm_gain)
    return (_encode(x_prompt, mem_prompt, wts), _encode(x_sample, mem_sample, wts))
```
